```python
import jax
import jax.numpy as jnp
from jax import lax
import numpy as np

D_MODEL = 1024
BATCH = 8
SEQ = 4096
DEPTH = 2
DEC_BATCH = 16
DEC_SEQ = 4096
PAST_LEN = 128

N_META = 16
GRID_W = 64
Q_BLOCK = 128
RMS_EPS = 1e-6
ROPE_THETA = 10000.0

MLA_HEADS = 4
MLA_Q_RANK = 384
MLA_KV_RANK = 256
MLA_NOPE = 128
MLA_ROPE = 64
MLA_V = 128
MLA_QK = MLA_NOPE + MLA_ROPE
CONV_DIM = D_MODEL // 2
CONV_WIDTH = 3
IN0_SPLITS = (MLA_Q_RANK, MLA_KV_RANK, MLA_ROPE, CONV_DIM, CONV_DIM, CONV_DIM)
IN0_DIM = MLA_Q_RANK + MLA_KV_RANK + MLA_ROPE + 3 * CONV_DIM
MIX0_DIM = MLA_HEADS * MLA_V + CONV_DIM

GQA_Q_HEADS = 8
GQA_KV_HEADS = 2
GQA_GROUP = GQA_Q_HEADS // GQA_KV_HEADS
GQA_HEAD_DIM = 128
AXIAL_DIM = GQA_HEAD_DIM // 2
IN1_DIM = (GQA_Q_HEADS + 2 * GQA_KV_HEADS) * GQA_HEAD_DIM
MIX1_DIM = GQA_Q_HEADS * GQA_HEAD_DIM

FFN_HIDDEN = ((8 * D_MODEL + 3 * 256 - 1) // (3 * 256)) * 256
N_EVEN = (DEPTH + 1) // 2
N_ODD = DEPTH // 2

kernel_name = 'hybrid_mla_conv_gqa_axial_encoder'


def rms_norm(x, g):
    xf = x.astype(jnp.float32)
    y = xf * lax.rsqrt(jnp.mean(xf * xf, axis=-1, keepdims=True) + RMS_EPS)
    return (y * g.astype(jnp.float32)).astype(x.dtype)


def rope_freqs(dim):
    return ROPE_THETA ** (-jnp.arange(0, dim, 2, dtype=jnp.float32) / dim)


def rotate(x, ang):
    half = x.shape[-1] // 2
    ang = ang.reshape((ang.shape[0],) + (1,) * (x.ndim - 3) + (half,))
    cos, sin = jnp.cos(ang), jnp.sin(ang)
    xf = x.astype(jnp.float32)
    x1, x2 = xf[..., :half], xf[..., half:]
    return jnp.concatenate([x1 * cos - x2 * sin, x2 * cos + x1 * sin], axis=-1).astype(x.dtype)


def line_angles(n_total):
    pos = jnp.arange(n_total, dtype=jnp.float32)
    return pos[:, None] * rope_freqs(MLA_ROPE)[None, :]


def axial_angles(n_tok):
    rows = n_tok // GRID_W
    row_idx = jnp.repeat(jnp.arange(rows, dtype=jnp.float32), GRID_W)
    col_idx = jnp.tile(jnp.arange(GRID_W, dtype=jnp.float32), rows)
    f = rope_freqs(AXIAL_DIM)
    meta = jnp.zeros((N_META, AXIAL_DIM // 2), jnp.float32)
    ang_r = jnp.concatenate([meta, row_idx[:, None] * f[None, :]], axis=0)
    ang_c = jnp.concatenate([meta, col_idx[:, None] * f[None, :]], axis=0)
    return ang_r, ang_c


def dense_bidir_attention(q, k, v):
    scale = q.shape[-1] ** -0.5

    def attend(qb):
        s = jnp.einsum('bqhgd,bkhd->bhgqk', qb, k, preferred_element_type=jnp.float32) * scale
        p = jax.nn.softmax(s, axis=-1).astype(v.dtype)
        return jnp.einsum('bhgqk,bkhd->bqhgd', p, v)

    b, n_total = q.shape[0], q.shape[1]
    n_tok = n_total - N_META
    n_blk = n_tok // Q_BLOCK
    out_meta = attend(q[:, :N_META])
    q_blocks = jnp.moveaxis(q[:, N_META:].reshape((b, n_blk, Q_BLOCK) + q.shape[2:]), 1, 0)
    out_blocks = lax.map(attend, q_blocks)
    out_tok = jnp.moveaxis(out_blocks, 0, 1).reshape((b, n_tok) + out_blocks.shape[3:])
    return jnp.concatenate([out_meta, out_tok], axis=1)


def mla_conv_mixer(h, w_in, q_a_g, kv_a_g, w_uq, w_ukv, q_g, k_g, conv_w, w_out):
    b, n_total, _ = h.shape
    z = h @ w_in
    c_q, c_kv, k_r, g_b, g_c, u = jnp.split(z, np.cumsum(IN0_SPLITS)[:-1].tolist(), axis=-1)
    q = (rms_norm(c_q, q_a_g) @ w_uq).reshape(b, n_total, MLA_HEADS, MLA_QK)
    kv = (rms_norm(c_kv, kv_a_g) @ w_ukv).reshape(b, n_total, MLA_HEADS, MLA_NOPE + MLA_V)
    k_nope, v = kv[..., :MLA_NOPE], kv[..., MLA_NOPE:]
    k_rope = jnp.broadcast_to(k_r[:, :, None, :], (b, n_total, MLA_HEADS, MLA_ROPE))
    k = jnp.concatenate([k_nope, k_rope], axis=-1)
    q = rms_norm(q, q_g)
    k = rms_norm(k, k_g)
    ang = line_angles(n_total)
    q = jnp.concatenate([q[..., :MLA_NOPE], rotate(q[..., MLA_NOPE:], ang)], axis=-1)
    k = jnp.concatenate([k[..., :MLA_NOPE], rotate(k[..., MLA_NOPE:], ang)], axis=-1)
    attn = dense_bidir_attention(q[:, :, :, None, :], k, v).reshape(b, n_total, MLA_HEADS * MLA_V)
    cu = g_c * u
    pad = (CONV_WIDTH - 1) // 2
    cu_p = jnp.pad(cu, ((0, 0), (pad, pad), (0, 0)))
    conv = cu_p[:, 0:n_total] * conv_w[0]
    for j in range(1, CONV_WIDTH):
        conv = conv + cu_p[:, j:j + n_total] * conv_w[j]
    y_conv = g_b * conv
    return jnp.concatenate([attn, y_conv], axis=-1) @ w_out


def gqa_axial_mixer(h, w_qkv, q_g, k_g, w_out):
    b, n_total, _ = h.shape
    qd = GQA_Q_HEADS * GQA_HEAD_DIM
    kd = GQA_KV_HEADS * GQA_HEAD_DIM
    qkv = h @ w_qkv
    q = qkv[..., :qd].reshape(b, n_total, GQA_KV_HEADS, GQA_GROUP, GQA_HEAD_DIM)
    k = qkv[..., qd:qd + kd].reshape(b, n_total, GQA_KV_HEADS, GQA_HEAD_DIM)
    v = qkv[..., qd + kd:].reshape(b, n_total, GQA_KV_HEADS, GQA_HEAD_DIM)
    q = rms_norm(q, q_g)
    k = rms_norm(k, k_g)
    ang_r, ang_c = axial_angles(n_total - N_META)
    q = jnp.concatenate([rotate(q[..., :AXIAL_DIM], ang_r), rotate(q[..., AXIAL_DIM:], ang_c)], axis=-1)
    k = jnp.concatenate([rotate(k[..., :AXIAL_DIM], ang_r), rotate(k[..., AXIAL_DIM:], ang_c)], axis=-1)
    o = dense_bidir_attention(q, k, v).reshape(b, n_total, MIX1_DIM)
    return o @ w_out


def swiglu(h, w1, w3, w2):
    return (jax.nn.silu(h @ w1) * (h @ w3)) @ w2


def trunk(x, meta_tokens, mix_norm_g, ffn_norm_g, mla_w_in, mla_q_a_g, mla_kv_a_g, mla_w_uq,
          mla_w_ukv, mla_q_g, mla_k_g, conv_w, even_w_out, gqa_w_qkv, gqa_q_g, gqa_k_g,
          odd_w_out, ffn_w1, ffn_w3, ffn_w2):
    b = x.shape[0]
    meta = jnp.broadcast_to(meta_tokens.astype(x.dtype)[None], (b, N_META, D_MODEL))
    h = jnp.concatenate([meta, x], axis=1)
    for l in range(DEPTH):
        i = l // 2
        hn = rms_norm(h, mix_norm_g[l])
        if l % 2 == 0:
            h = h + mla_conv_mixer(hn, mla_w_in[i], mla_q_a_g[i], mla_kv_a_g[i], mla_w_uq[i],
                                   mla_w_ukv[i], mla_q_g[i], mla_k_g[i], conv_w[i], even_w_out[i])
        else:
            h = h + gqa_axial_mixer(hn, gqa_w_qkv[i], gqa_q_g[i], gqa_k_g[i], odd_w_out[i])
        h = h + swiglu(rms_norm(h, ffn_norm_g[l]), ffn_w1[l], ffn_w3[l], ffn_w2[l])
    return h[:, N_META:]


def setup_inputs(seed: int = 0) -> dict:
    key = jax.random.key(seed)
    ks = jax.random.split(key, 22)

    def dense(k, shape, fan_in):
        return jax.random.normal(k, shape, jnp.float32) * fan_in ** -0.5

    def gain(k, shape):
        return 1.0 + 0.02 * jax.random.normal(k, shape, jnp.float32)

    return {
        'x_prompt': jax.random.normal(ks[0], (BATCH, SEQ, D_MODEL), jnp.float32),
        'x_sample': jax.random.normal(ks[1], (DEC_BATCH, DEC_SEQ, D_MODEL), jnp.float32),
        'meta_tokens': jax.random.normal(ks[2], (N_META, D_MODEL), jnp.float32),
        'mix_norm_g': gain(ks[3], (DEPTH, D_MODEL)),
        'ffn_norm_g': gain(ks[4], (DEPTH, D_MODEL)),
        'mla_w_in': dense(ks[5], (N_EVEN, D_MODEL, IN0_DIM), D_MODEL),
        'mla_q_a_g': gain(ks[6], (N_EVEN, MLA_Q_RANK)),
        'mla_kv_a_g': gain(ks[7], (N_EVEN, MLA_KV_RANK)),
        'mla_w_uq': dense(ks[8], (N_EVEN, MLA_Q_RANK, MLA_HEADS * MLA_QK), MLA_Q_RANK),
        'mla_w_ukv': dense(ks[9], (N_EVEN, MLA_KV_RANK, MLA_HEADS * (MLA_NOPE + MLA_V)), MLA_KV_RANK),
        'mla_q_g': gain(ks[10], (N_EVEN, MLA_QK)),
        'mla_k_g': gain(ks[11], (N_EVEN, MLA_QK)),
        'conv_w': dense(ks[12], (N_EVEN, CONV_WIDTH, CONV_DIM), CONV_WIDTH),
        'even_w_out': dense(ks[13], (N_EVEN, MIX0_DIM, D_MODEL), MIX0_DIM),
        'gqa_w_qkv': dense(ks[14], (N_ODD, D_MODEL, IN1_DIM), D_MODEL),
        'gqa_q_g': gain(ks[15], (N_ODD, GQA_HEAD_DIM)),
        'gqa_k_g': gain(ks[16], (N_ODD, GQA_HEAD_DIM)),
        'odd_w_out': dense(ks[17], (N_ODD, MIX1_DIM, D_MODEL), MIX1_DIM),
        'ffn_w1': dense(ks[18], (DEPTH, D_MODEL, FFN_HIDDEN), D_MODEL),
        'ffn_w3': dense(ks[19], (DEPTH, D_MODEL, FFN_HIDDEN), D_MODEL),
        'ffn_w2': dense(ks[20], (DEPTH, FFN_HIDDEN, D_MODEL), FFN_HIDDEN),
    }


def reference(x_prompt, x_sample, meta_tokens, mix_norm_g, ffn_norm_g, mla_w_in, mla_q_a_g,
              mla_kv_a_g, mla_w_uq, mla_w_ukv, mla_q_g, mla_k_g, conv_w, even_w_out, gqa_w_qkv,
              gqa_q_g, gqa_k_g, odd_w_out, ffn_w1, ffn_w3, ffn_w2):
    y_prompt = trunk(x_prompt, meta_tokens, mix_norm_g, ffn_norm_g, mla_w_in, mla_q_a_g, mla_kv_a_g,
                     mla_w_uq, mla_w_ukv, mla_q_g, mla_k_g, conv_w, even_w_out, gqa_w_qkv, gqa_q_g,
                     gqa_k_g, odd_w_out, ffn_w1, ffn_w3, ffn_w2)
    y_sample = trunk(x_sample, meta_tokens, mix_norm_g, ffn_norm_g, mla_w_in, mla_q_a_g, mla_kv_a_g,
                     mla_w_uq, mla_w_ukv, mla_q_g, mla_k_g, conv_w, even_w_out, gqa_w_qkv, gqa_q_g,
                     gqa_k_g, odd_w_out, ffn_w1, ffn_w3, ffn_w2)
    return (y_prompt, y_sample)
```

```python
import functools
import math

import jax
import jax.numpy as jnp
import numpy as np
from jax import lax
from jax.experimental import pallas as pl
from jax.experimental.pallas import tpu as pltpu

F32 = jnp.float32
BF16 = jnp.bfloat16

D_MODEL = 1024
N_META = 16
GRID_W = 64
RMS_EPS = 1e-6
ROPE_THETA = 10000.0

MLA_HEADS = 4
MLA_Q_RANK = 384
MLA_KV_RANK = 256
MLA_NOPE = 128
MLA_ROPE = 64
MLA_V = 128
MLA_QK = MLA_NOPE + MLA_ROPE
CONV_DIM = D_MODEL // 2
CONV_WIDTH = 3

GQA_Q_HEADS = 8
GQA_KV_HEADS = 2
GQA_GROUP = GQA_Q_HEADS // GQA_KV_HEADS
GQA_HEAD_DIM = 128
AXIAL_DIM = GQA_HEAD_DIM // 2

FFN_HIDDEN = ((8 * D_MODEL + 3 * 256 - 1) // (3 * 256)) * 256

LANE = 128
BF16_SUBLANES = 16
MLA_HEAD_PAD = 2 * LANE
V_ROWS = MLA_V + BF16_SUBLANES
IN0_PAD = MLA_Q_RANK + MLA_KV_RANK + LANE + 3 * CONV_DIM
LOG2E = math.log2(math.e)
VMEM_LIMIT = 56 * 1024 * 1024

ROW_TILE = 512
Q_TILE = 256
FFN_CHUNK = FFN_HIDDEN // 2


def _params(n_axes):
    return pltpu.CompilerParams(dimension_semantics=("arbitrary",) * n_axes,
                                vmem_limit_bytes=VMEM_LIMIT)


def _resident(shape):
    nd = len(shape)
    return pl.BlockSpec(shape, lambda *_: (0,) * nd, pipeline_mode=pl.Buffered(1))


def _rms(x, g):
    ms = jnp.mean(x * x, axis=-1, keepdims=True)
    return x * lax.rsqrt(ms + RMS_EPS) * g


def _rope(x, cos, sin_a, sin_b):
    return x * cos + pltpu.roll(x, 32, 1) * sin_a + pltpu.roll(x, 96, 1) * sin_b


def _ones_rows(n):
    row = lax.broadcasted_iota(jnp.int32, (BF16_SUBLANES, n), 0)
    return jnp.where(row == 0, 1.0, 0.0).astype(BF16)


def _l0_proj_kernel(h_ref, gmix_ref, win_ref, qag_ref, kvag_ref, wuq_ref, wuk_ref, wuv_ref,
                    qg_ref, kg_ref, cos_ref, sa_ref, sb_ref,
                    q_ref, k_ref, vt_ref, gb_ref, cu_ref):
    hn = _rms(h_ref[...], gmix_ref[...]).astype(BF16)
    z = jnp.dot(hn, win_ref[...], preferred_element_type=F32)
    o = 0
    c_q = z[:, o:o + MLA_Q_RANK]; o += MLA_Q_RANK
    c_kv = z[:, o:o + MLA_KV_RANK]; o += MLA_KV_RANK
    k_r = z[:, o:o + LANE]; o += LANE
    g_b = z[:, o:o + CONV_DIM]; o += CONV_DIM
    g_c = z[:, o:o + CONV_DIM]; o += CONV_DIM
    u = z[:, o:o + CONV_DIM]
    gb_ref[...] = g_b
    cu_ref[...] = g_c * u

    cos, sa, sb = cos_ref[...], sa_ref[...], sb_ref[...]
    q = jnp.dot(_rms(c_q, qag_ref[...]).astype(BF16), wuq_ref[...], preferred_element_type=F32)
    ckv = _rms(c_kv, kvag_ref[...]).astype(BF16)
    kn = jnp.dot(ckv, wuk_ref[...], preferred_element_type=F32)
    v = jnp.dot(ckv, wuv_ref[...], preferred_element_type=F32)

    qg_n, qg_r = qg_ref[:, :LANE], qg_ref[:, LANE:]
    kg_n, kg_r = kg_ref[:, :LANE], kg_ref[:, LANE:]
    q_scale = MLA_QK ** -0.5 * LOG2E
    kr_sq = jnp.sum(k_r * k_r, axis=-1, keepdims=True)
    kr_rot = _rope(k_r * kg_r, cos, sa, sb)
    tm = z.shape[0]
    ones = _ones_rows(tm)
    for h in range(MLA_HEADS):
        c = h * MLA_HEAD_PAD
        qn, qr = q[:, c:c + LANE], q[:, c + LANE:c + 2 * LANE]
        ms = (jnp.sum(qn * qn, axis=-1, keepdims=True) + jnp.sum(qr * qr, axis=-1, keepdims=True)) / MLA_QK
        r = lax.rsqrt(ms + RMS_EPS)
        q_ref[:, c:c + LANE] = (qn * r * qg_n * q_scale).astype(BF16)
        q_ref[:, c + LANE:c + 2 * LANE] = (_rope(qr * r * qg_r, cos, sa, sb) * q_scale).astype(BF16)
        kh = kn[:, h * LANE:(h + 1) * LANE]
        ms = (jnp.sum(kh * kh, axis=-1, keepdims=True) + kr_sq) / MLA_QK
        r = lax.rsqrt(ms + RMS_EPS)
        k_ref[:, c:c + LANE] = (kh * r * kg_n).astype(BF16)
        k_ref[:, c + LANE:c + 2 * LANE] = (kr_rot * r).astype(BF16)
        vt_ref[h, :MLA_V, :] = v[:, h * LANE:(h + 1) * LANE].T.astype(BF16)
        vt_ref[h, MLA_V:, :] = ones


def _l0_proj(h, tabs, w, tm):
    rows = h.shape[0]
    n_tab = tabs[0].shape[0] // tm
    row = lambda n: pl.BlockSpec((tm, n), lambda i: (i, 0))
    tab = pl.BlockSpec((tm, LANE), lambda i: (i % n_tab, 0))
    ins = [h, w["g_mix"], w["w_in"], w["q_a_g"], w["kv_a_g"], w["w_uq"], w["w_uk"], w["w_uv"],
           w["q_g"], w["k_g"], *tabs]
    in_specs = [row(D_MODEL)] + [_resident(a.shape) for a in ins[1:10]] + [tab] * 3
    out_shape = [jax.ShapeDtypeStruct((rows, MLA_HEADS * MLA_HEAD_PAD), BF16),
                 jax.ShapeDtypeStruct((rows, MLA_HEADS * MLA_HEAD_PAD), BF16),
                 jax.ShapeDtypeStruct((MLA_HEADS, V_ROWS, rows), BF16),
                 jax.ShapeDtypeStruct((rows, CONV_DIM), F32),
                 jax.ShapeDtypeStruct((rows, CONV_DIM), F32)]
    out_specs = [row(MLA_HEADS * MLA_HEAD_PAD), row(MLA_HEADS * MLA_HEAD_PAD),
                 pl.BlockSpec((MLA_HEADS, V_ROWS, tm), lambda i: (0, 0, i)),
                 row(CONV_DIM), row(CONV_DIM)]
    return pl.pallas_call(_l0_proj_kernel, grid=(rows // tm,), in_specs=in_specs,
                          out_specs=out_specs, out_shape=out_shape,
                          compiler_params=_params(1), name="l0_proj")(*ins)


def _l1_proj_kernel(h_ref, gmix_ref, wqkv_ref, qg_ref, kg_ref, cos_ref, sa_ref, sb_ref,
                    q_ref, k_ref, vt_ref):
    hn = _rms(h_ref[...], gmix_ref[...]).astype(BF16)
    z = jnp.dot(hn, wqkv_ref[...], preferred_element_type=F32)
    cos, sa, sb = cos_ref[...], sa_ref[...], sb_ref[...]
    q_scale = GQA_HEAD_DIM ** -0.5 * LOG2E
    ones = _ones_rows(z.shape[0])
    for h in range(GQA_Q_HEADS):
        x = z[:, h * LANE:(h + 1) * LANE]
        q_ref[:, h * LANE:(h + 1) * LANE] = (
            _rope(_rms(x, qg_ref[...]), cos, sa, sb) * q_scale).astype(BF16)
    for h in range(GQA_KV_HEADS):
        c = (GQA_Q_HEADS + h) * LANE
        k_ref[:, h * LANE:(h + 1) * LANE] = _rope(_rms(z[:, c:c + LANE], kg_ref[...]), cos, sa, sb).astype(BF16)
        c = (GQA_Q_HEADS + GQA_KV_HEADS + h) * LANE
        vt_ref[h, :GQA_HEAD_DIM, :] = z[:, c:c + LANE].T.astype(BF16)
        vt_ref[h, GQA_HEAD_DIM:, :] = ones


def _l1_proj(h, tabs, w, tm):
    rows = h.shape[0]
    n_tab = tabs[0].shape[0] // tm
    row = lambda n: pl.BlockSpec((tm, n), lambda i: (i, 0))
    tab = pl.BlockSpec((tm, LANE), lambda i: (i % n_tab, 0))
    ins = [h, w["g_mix"], w["w_qkv"], w["q_g"], w["k_g"], *tabs]
    in_specs = [row(D_MODEL)] + [_resident(a.shape) for a in ins[1:5]] + [tab] * 3
    out_shape = [jax.ShapeDtypeStruct((rows, GQA_Q_HEADS * LANE), BF16),
                 jax.ShapeDtypeStruct((rows, GQA_KV_HEADS * LANE), BF16),
                 jax.ShapeDtypeStruct((GQA_KV_HEADS, V_ROWS, rows), BF16)]
    out_specs = [row(GQA_Q_HEADS * LANE), row(GQA_KV_HEADS * LANE),
                 pl.BlockSpec((GQA_KV_HEADS, V_ROWS, tm), lambda i: (0, 0, i))]
    return pl.pallas_call(_l1_proj_kernel, grid=(rows // tm,), in_specs=in_specs,
                          out_specs=out_specs, out_shape=out_shape,
                          compiler_params=_params(1), name="l1_proj")(*ins)


def _attn_kernel(q_ref, k_ref, vt_ref, qm_ref, km_ref, vtm_ref, o_ref, om_ref, *, tq):
    dv = o_ref.shape[-1]
    nt = (((1,), (1,)), ((), ()))

    def attend(q):
        s = lax.dot_general(k_ref[...], q, nt, preferred_element_type=F32)
        sm = lax.dot_general(km_ref[...], q, nt, preferred_element_type=F32)
        m = jnp.maximum(jnp.max(s, axis=0, keepdims=True), jnp.max(sm, axis=0, keepdims=True))
        p = jnp.exp2(s - m).astype(BF16)
        pm = jnp.exp2(sm - m).astype(BF16)
        ot = (jnp.dot(vt_ref[0], p, preferred_element_type=F32)
              + jnp.dot(vtm_ref[0, 0], pm, preferred_element_type=F32))
        return (ot[:dv] * (1.0 / ot[dv:dv + 1])).T

    def body(i, carry):
        r = pl.multiple_of(i * tq, tq)
        o_ref[pl.ds(r, tq), :] = attend(q_ref[pl.ds(r, tq), :]).astype(o_ref.dtype)
        return carry

    lax.fori_loop(0, q_ref.shape[0] // tq, body, 0)
    qm = qm_ref[...]
    qm = jnp.concatenate([qm, jnp.zeros((LANE - qm.shape[0], qm.shape[1]), qm.dtype)], axis=0)
    om_ref[...] = attend(qm)[:om_ref.shape[0]].astype(om_ref.dtype)


def _attention(q, k, vt, qm, km, vtm, *, batch, seq, n_q_heads, group, dk, dv, tq):
    out_shape = [jax.ShapeDtypeStruct((batch * seq, n_q_heads * dv), BF16),
                 jax.ShapeDtypeStruct((batch * N_META, n_q_heads * dv), BF16)]
    in_specs = [pl.BlockSpec((seq, dk), lambda b, h: (b, h)),
                pl.BlockSpec((seq, dk), lambda b, h: (b, h // group)),
                pl.BlockSpec((1, V_ROWS, seq), lambda b, h: (h // group, 0, b)),
                pl.BlockSpec((N_META, dk), lambda b, h: (b, h)),
                pl.BlockSpec((N_META, dk), lambda b, h: (b, h // group)),
                pl.BlockSpec((1, 1, V_ROWS, N_META), lambda b, h: (b, h // group, 0, 0))]
    out_specs = [pl.BlockSpec((seq, dv), lambda b, h: (b, h)),
                 pl.BlockSpec((N_META, dv), lambda b, h: (b, h))]
    return pl.pallas_call(functools.partial(_attn_kernel, tq=tq), grid=(batch, n_q_heads),
                          in_specs=in_specs, out_specs=out_specs, out_shape=out_shape,
                          compiler_params=_params(2), name="attention")(q, k, vt, qm, km, vtm)


def _ffn_tail(h1, gffn_ref, w1_ref, w3_ref, w2_ref):
    hn = _rms(h1, gffn_ref[...]).astype(BF16)
    out = h1
    for c in range(0, FFN_HIDDEN, FFN_CHUNK):
        a = jnp.dot(hn, w1_ref[:, c:c + FFN_CHUNK], preferred_element_type=F32)
        b = jnp.dot(hn, w3_ref[:, c:c + FFN_CHUNK], preferred_element_type=F32)
        g = (a * jax.nn.sigmoid(a) * b).astype(BF16)
        out = out + jnp.dot(g, w2_ref[c:c + FFN_CHUNK, :], preferred_element_type=F32)
    return out


def _conv_gate(gb, cu, left_edge, right_edge, first, last, cw_ref):
    n = cu.shape[0]
    left = jnp.where(first, left_edge, pltpu.roll(cu, 1, 0))
    right = jnp.where(last, right_edge, pltpu.roll(cu, n - 1, 0))
    conv = left * cw_ref[0:1, :] + cu * cw_ref[1:2, :] + right * cw_ref[2:3, :]
    return gb * conv


def _l0_out_seq_kernel(h_ref, at_ref, gb_ref, cu_ref, prev_ref, next_ref, cum_ref, cw_ref,
                       woa_ref, woc_ref, gffn_ref, w1_ref, w3_ref, w2_ref, o_ref, *, tiles_per_seq):
    j = pl.program_id(0) % tiles_per_seq
    cu = cu_ref[...]
    tm = cu.shape[0]
    row = lax.broadcasted_iota(jnp.int32, (tm, 1), 0)
    left_edge = jnp.where(j == 0, cum_ref[N_META - 1:N_META, :], prev_ref[7:8, :])
    right_edge = jnp.where(j == tiles_per_seq - 1, 0.0, next_ref[0:1, :])
    y = _conv_gate(gb_ref[...], cu, left_edge, right_edge, row == 0, row == tm - 1, cw_ref)
    mix = (jnp.dot(at_ref[...], woa_ref[...], preferred_element_type=F32)
           + jnp.dot(y.astype(BF16), woc_ref[...], preferred_element_type=F32))
    o_ref[...] = _ffn_tail(h_ref[...] + mix, gffn_ref, w1_ref, w3_ref, w2_ref)


def _l0_out_meta_kernel(h_ref, at_ref, gb_ref, cu_ref, nxt_ref, cw_ref,
                        woa_ref, woc_ref, gffn_ref, w1_ref, w3_ref, w2_ref, o_ref):
    cu = cu_ref[...]
    pos = lax.broadcasted_iota(jnp.int32, (cu.shape[0], 1), 0) % N_META
    y = _conv_gate(gb_ref[...], cu, 0.0, nxt_ref[...], pos == 0, pos == N_META - 1, cw_ref)
    mix = (jnp.dot(at_ref[...], woa_ref[...], preferred_element_type=F32)
           + jnp.dot(y.astype(BF16), woc_ref[...], preferred_element_type=F32))
    o_ref[...] = _ffn_tail(h_ref[...] + mix, gffn_ref, w1_ref, w3_ref, w2_ref)


def _l1_out_kernel(h_ref, at_ref, wo_ref, gffn_ref, w1_ref, w3_ref, w2_ref, o_ref):
    mix = jnp.dot(at_ref[...], wo_ref[...], preferred_element_type=F32)
    o_ref[...] = _ffn_tail(h_ref[...] + mix, gffn_ref, w1_ref, w3_ref, w2_ref)


def _ffn_weights(w):
    return [w["g_ffn"], w["w1"], w["w3"], w["w2"]]


def _l0_out_seq(h, attn, gb, cu, cu_meta, w, *, seq, tm):
    rows = h.shape[0]
    tiles_per_seq = seq // tm
    sub = tm // 8
    n_sub = rows // 8
    row = lambda n: pl.BlockSpec((tm, n), lambda i: (i, 0))
    weights = [w["conv_w"], w["w_out_a"], w["w_out_c"]] + _ffn_weights(w)
    in_specs = [row(D_MODEL), row(CONV_DIM), row(CONV_DIM), row(CONV_DIM),
                pl.BlockSpec((8, CONV_DIM), lambda i: (jnp.maximum(i * sub - 1, 0), 0)),
                pl.BlockSpec((8, CONV_DIM), lambda i: (jnp.minimum((i + 1) * sub, n_sub - 1), 0)),
                pl.BlockSpec((N_META, CONV_DIM), lambda i: (i // tiles_per_seq, 0))]
    in_specs += [_resident(a.shape) for a in weights]
    return pl.pallas_call(functools.partial(_l0_out_seq_kernel, tiles_per_seq=tiles_per_seq),
                          grid=(rows // tm,), in_specs=in_specs, out_specs=row(D_MODEL),
                          out_shape=jax.ShapeDtypeStruct((rows, D_MODEL), F32),
                          compiler_params=_params(1), name="l0_out_seq")(
                              h, attn, gb, cu, cu, cu, cu_meta, *weights)


def _l0_out_meta(h, attn, gb, cu, nxt, w):
    rows = h.shape[0]
    weights = [w["conv_w"], w["w_out_a"], w["w_out_c"]] + _ffn_weights(w)
    ins = [h, attn, gb, cu, nxt, *weights]
    return pl.pallas_call(_l0_out_meta_kernel, grid=(1,),
                          in_specs=[_resident(a.shape) for a in ins],
                          out_specs=pl.BlockSpec((rows, D_MODEL), lambda i: (0, 0)),
                          out_shape=jax.ShapeDtypeStruct((rows, D_MODEL), F32),
                          compiler_params=_params(1), name="l0_out_meta")(*ins)


def _l1_out(h, attn, w, tm):
    rows = h.shape[0]
    row = lambda n: pl.BlockSpec((tm, n), lambda i: (i, 0))
    weights = [w["w_out"]] + _ffn_weights(w)
    return pl.pallas_call(_l1_out_kernel, grid=(rows // tm,),
                          in_specs=[row(D_MODEL), row(D_MODEL)] + [_resident(a.shape) for a in weights],
                          out_specs=row(D_MODEL),
                          out_shape=jax.ShapeDtypeStruct((rows, D_MODEL), F32),
                          compiler_params=_params(1), name="l1_out")(h, attn, *weights)


def _rope_freqs(dim):
    return ROPE_THETA ** (-jnp.arange(0, dim, 2, dtype=F32) / dim)


def _rope_tables(ang_a, ang_b):
    z = jnp.zeros_like(ang_a)
    ca, sa, cb, sb = jnp.cos(ang_a), jnp.sin(ang_a), jnp.cos(ang_b), jnp.sin(ang_b)
    return (jnp.concatenate([ca, ca, cb, cb], axis=1),
            jnp.concatenate([z, sa, z, sb], axis=1),
            jnp.concatenate([-sa, z, -sb, z], axis=1))


def _line_tables(seq):
    pos = jnp.arange(N_META + seq, dtype=F32)
    ang = pos[:, None] * _rope_freqs(MLA_ROPE)[None, :]
    cos, sa, sb = _rope_tables(ang, jnp.zeros_like(ang))
    keep = (jnp.arange(LANE) < MLA_ROPE).astype(F32)[None, :]
    return tuple(t * keep for t in (cos, sa, sb))


def _axial_tables(seq):
    rows = seq // GRID_W
    f = _rope_freqs(AXIAL_DIM)
    row_idx = jnp.repeat(jnp.arange(rows, dtype=F32), GRID_W)
    col_idx = jnp.tile(jnp.arange(GRID_W, dtype=F32), rows)
    meta = jnp.zeros((N_META, AXIAL_DIM // 2), F32)
    ang_r = jnp.concatenate([meta, row_idx[:, None] * f[None, :]], axis=0)
    ang_c = jnp.concatenate([meta, col_idx[:, None] * f[None, :]], axis=0)
    return _rope_tables(ang_r, ang_c)


def _split_tables(tabs, batch):
    seq_t = tuple(t[N_META:] for t in tabs)
    meta_t = tuple(jnp.tile(t[:N_META], (batch, 1)) for t in tabs)
    return seq_t, meta_t


def _pad_gain(g):
    return jnp.concatenate([g, jnp.zeros((MLA_HEAD_PAD - MLA_QK,), g.dtype)])[None, :]


def _prep_even(i, l, p):
    w_in = p["mla_w_in"][i]
    cut = MLA_Q_RANK + MLA_KV_RANK + MLA_ROPE
    w_in = jnp.concatenate([w_in[:, :cut], jnp.zeros((D_MODEL, LANE - MLA_ROPE), F32), w_in[:, cut:]], axis=1)
    w_uq = p["mla_w_uq"][i].reshape(MLA_Q_RANK, MLA_HEADS, MLA_QK)
    w_uq = jnp.pad(w_uq, ((0, 0), (0, 0), (0, MLA_HEAD_PAD - MLA_QK))).reshape(MLA_Q_RANK, -1)
    w_ukv = p["mla_w_ukv"][i].reshape(MLA_KV_RANK, MLA_HEADS, MLA_NOPE + MLA_V)
    w_out = p["even_w_out"][i]
    return {
        "g_mix": p["mix_norm_g"][l][None, :], "w_in": w_in.astype(BF16),
        "q_a_g": p["mla_q_a_g"][i][None, :], "kv_a_g": p["mla_kv_a_g"][i][None, :],
        "w_uq": w_uq.astype(BF16),
        "w_uk": w_ukv[:, :, :MLA_NOPE].reshape(MLA_KV_RANK, -1).astype(BF16),
        "w_uv": w_ukv[:, :, MLA_NOPE:].reshape(MLA_KV_RANK, -1).astype(BF16),
        "q_g": _pad_gain(p["mla_q_g"][i]), "k_g": _pad_gain(p["mla_k_g"][i]),
        "conv_w": p["conv_w"][i],
        "w_out_a": w_out[:MLA_HEADS * MLA_V].astype(BF16), "w_out_c": w_out[MLA_HEADS * MLA_V:].astype(BF16),
        "g_ffn": p["ffn_norm_g"][l][None, :], "w1": p["ffn_w1"][l].astype(BF16),
        "w3": p["ffn_w3"][l].astype(BF16), "w2": p["ffn_w2"][l].astype(BF16),
    }


def _prep_odd(i, l, p):
    return {
        "g_mix": p["mix_norm_g"][l][None, :], "w_qkv": p["gqa_w_qkv"][i].astype(BF16),
        "q_g": p["gqa_q_g"][i][None, :], "k_g": p["gqa_k_g"][i][None, :],
        "w_out": p["odd_w_out"][i].astype(BF16),
        "g_ffn": p["ffn_norm_g"][l][None, :], "w1": p["ffn_w1"][l].astype(BF16),
        "w3": p["ffn_w3"][l].astype(BF16), "w2": p["ffn_w2"][l].astype(BF16),
    }


def _meta_vt(vt, batch):
    hkv = vt.shape[0]
    return vt.reshape(hkv, V_ROWS, batch, N_META).transpose(2, 0, 1, 3)


def _trunk(x, meta_tokens, layers, line_tabs, axial_tabs):
    batch, seq, _ = x.shape
    tm = min(ROW_TILE, seq)
    tq = min(Q_TILE, seq)
    hs = x.reshape(batch * seq, D_MODEL)
    hm = jnp.tile(meta_tokens.astype(x.dtype), (batch, 1))
    rows_m = batch * N_META
    for kind, w in layers:
        if kind == "even":
            tabs_s, tabs_m = _split_tables(line_tabs, batch)
            q, k, vt, gb, cu = _l0_proj(hs, tabs_s, w, tm)
            qm, km, vtm, gbm, cum = _l0_proj(hm, tabs_m, w, rows_m)
            o, om = _attention(q, k, vt, qm, km, _meta_vt(vtm, batch), batch=batch, seq=seq,
                               n_q_heads=MLA_HEADS, group=1, dk=MLA_HEAD_PAD, dv=MLA_V, tq=tq)
            first = cu.reshape(batch, seq, CONV_DIM)[:, 0]
            nxt = jnp.zeros((batch, N_META, CONV_DIM), F32).at[:, N_META - 1].set(first)
            hs_new = _l0_out_seq(hs, o, gb, cu, cum, w, seq=seq, tm=tm)
            hm = _l0_out_meta(hm, om, gbm, cum, nxt.reshape(rows_m, CONV_DIM), w)
            hs = hs_new
        else:
            tabs_s, tabs_m = _split_tables(axial_tabs, batch)
            q, k, vt = _l1_proj(hs, tabs_s, w, tm)
            qm, km, vtm = _l1_proj(hm, tabs_m, w, rows_m)
            o, om = _attention(q, k, vt, qm, km, _meta_vt(vtm, batch), batch=batch, seq=seq,
                               n_q_heads=GQA_Q_HEADS, group=GQA_GROUP, dk=GQA_HEAD_DIM,
                               dv=GQA_HEAD_DIM, tq=tq)
            hs = _l1_out(hs, o, w, tm)
            hm = _l1_out(hm, om, w, rows_m)
    return hs.reshape(batch, seq, D_MODEL)


def kernel(x_prompt, x_sample, meta_tokens, mix_norm_g, ffn_norm_g, mla_w_in, mla_q_a_g, mla_kv_a_g, mla_w_uq, mla_w_ukv, mla_q_g, mla_k_g, conv_w, even_w_out, gqa_w_qkv, gqa_q_g, gqa_k_g, odd_w_out, ffn_w1, ffn_w3, ffn_w2):
    p = dict(mix_norm_g=mix_norm_g, ffn_norm_g=ffn_norm_g, mla_w_in=mla_w_in, mla_q_a_g=mla_q_a_g,
             mla_kv_a_g=mla_kv_a_g, mla_w_uq=mla_w_uq, mla_w_ukv=mla_w_ukv, mla_q_g=mla_q_g,
             mla_k_g=mla_k_g, conv_w=conv_w, even_w_out=even_w_out, gqa_w_qkv=gqa_w_qkv,
             gqa_q_g=gqa_q_g, gqa_k_g=gqa_k_g, odd_w_out=odd_w_out, ffn_w1=ffn_w1, ffn_w3=ffn_w3,
             ffn_w2=ffn_w2)
    depth = mix_norm_g.shape[0]
    layers = [("even", _prep_even(l // 2, l, p)) if l % 2 == 0 else ("odd", _prep_odd(l // 2, l, p))
              for l in range(depth)]
    outs = []
    for x in (x_prompt, x_sample):
        seq = x.shape[1]
        outs.append(_trunk(x, meta_tokens, layers, _line_tables(seq), _axial_tables(seq)))
    return tuple(outs)
```

```python
import functools
import math

import jax
import jax.numpy as jnp
from jax import lax
from jax.experimental import pallas as pl
from jax.experimental.pallas import tpu as pltpu

F32 = jnp.float32
BF16 = jnp.bfloat16

D_MODEL = 1024
N_META = 16
GRID_W = 64
RMS_EPS = 1e-6
ROPE_THETA = 10000.0

MLA_HEADS = 4
MLA_Q_RANK = 384
MLA_KV_RANK = 256
MLA_NOPE = 128
MLA_ROPE = 64
MLA_V = 128
MLA_QK = MLA_NOPE + MLA_ROPE
CONV_DIM = D_MODEL // 2
CONV_WIDTH = 3

GQA_Q_HEADS = 8
GQA_KV_HEADS = 2
GQA_GROUP = GQA_Q_HEADS // GQA_KV_HEADS
GQA_HEAD_DIM = 128
AXIAL_DIM = GQA_HEAD_DIM // 2

FFN_HIDDEN = ((8 * D_MODEL + 3 * 256 - 1) // (3 * 256)) * 256

LANE = 128
SUBLANE = 8
BF16_SUBLANES = 16
MLA_HEAD_PAD = 2 * LANE
V_ROWS = MLA_V + BF16_SUBLANES
LOG2E = math.log2(math.e)
VMEM_LIMIT = 56 * 1024 * 1024

ROW_TILE = 512
Q_TILE = 256
KEY_CHUNK = 256
FFN_CHUNK = FFN_HIDDEN // 2


def _params(n_axes):
    return pltpu.CompilerParams(dimension_semantics=("arbitrary",) * n_axes,
                                vmem_limit_bytes=VMEM_LIMIT)


def _resident(shape):
    nd = len(shape)
    return pl.BlockSpec(shape, lambda *_: (0,) * nd, pipeline_mode=pl.Buffered(1))


def _rms(x, g):
    ms = jnp.mean(x * x, axis=-1, keepdims=True)
    return x * lax.rsqrt(ms + RMS_EPS) * g


def _rope(x, cos, sin_a, sin_b):
    return x * cos + pltpu.roll(x, 32, 1) * sin_a + pltpu.roll(x, 96, 1) * sin_b


def _ones_rows(n):
    row = lax.broadcasted_iota(jnp.int32, (BF16_SUBLANES, n), 0)
    return jnp.where(row == 0, 1.0, 0.0).astype(BF16)


def _l0_proj_kernel(h_ref, gmix_ref, win_ref, qag_ref, kvag_ref, wuq_ref, wuk_ref, wuv_ref,
                    qg_ref, kg_ref, cos_ref, sa_ref, sb_ref,
                    q_ref, k_ref, vt_ref, gb_ref, cu_ref):
    hn = _rms(h_ref[...], gmix_ref[...]).astype(BF16)
    z = jnp.dot(hn, win_ref[...], preferred_element_type=F32)
    o = 0
    c_q = z[:, o:o + MLA_Q_RANK]; o += MLA_Q_RANK
    c_kv = z[:, o:o + MLA_KV_RANK]; o += MLA_KV_RANK
    k_r = z[:, o:o + LANE]; o += LANE
    g_b = z[:, o:o + CONV_DIM]; o += CONV_DIM
    g_c = z[:, o:o + CONV_DIM]; o += CONV_DIM
    u = z[:, o:o + CONV_DIM]
    gb_ref[...] = g_b
    cu_ref[...] = g_c * u

    cos, sa, sb = cos_ref[...], sa_ref[...], sb_ref[...]
    q = jnp.dot(_rms(c_q, qag_ref[...]).astype(BF16), wuq_ref[...], preferred_element_type=F32)
    ckv = _rms(c_kv, kvag_ref[...]).astype(BF16)
    kn = jnp.dot(ckv, wuk_ref[...], preferred_element_type=F32)
    v = jnp.dot(ckv, wuv_ref[...], preferred_element_type=F32)

    qg_n, qg_r = qg_ref[:, :LANE], qg_ref[:, LANE:]
    kg_n, kg_r = kg_ref[:, :LANE], kg_ref[:, LANE:]
    q_scale = MLA_QK ** -0.5 * LOG2E
    kr_sq = jnp.sum(k_r * k_r, axis=-1, keepdims=True)
    kr_rot = _rope(k_r * kg_r, cos, sa, sb)
    ones = _ones_rows(z.shape[0])
    for h in range(MLA_HEADS):
        c = h * MLA_HEAD_PAD
        qn, qr = q[:, c:c + LANE], q[:, c + LANE:c + 2 * LANE]
        ms = (jnp.sum(qn * qn, axis=-1, keepdims=True) + jnp.sum(qr * qr, axis=-1, keepdims=True)) / MLA_QK
        r = lax.rsqrt(ms + RMS_EPS)
        q_ref[0, h, :, :LANE] = (qn * r * qg_n * q_scale).astype(BF16)
        q_ref[0, h, :, LANE:] = (_rope(qr * r * qg_r, cos, sa, sb) * q_scale).astype(BF16)
        kh = kn[:, h * LANE:(h + 1) * LANE]
        ms = (jnp.sum(kh * kh, axis=-1, keepdims=True) + kr_sq) / MLA_QK
        r = lax.rsqrt(ms + RMS_EPS)
        k_ref[0, h, :, :LANE] = (kh * r * kg_n).astype(BF16)
        k_ref[0, h, :, LANE:] = (kr_rot * r).astype(BF16)
        vt_ref[0, h, :MLA_V, :] = v[:, h * LANE:(h + 1) * LANE].T.astype(BF16)
        vt_ref[0, h, MLA_V:, :] = ones


def _head_specs(n_heads, tm, dk):
    return pl.BlockSpec((1, n_heads, tm, dk), lambda b, j: (b, 0, j, 0))


def _vt_spec(n_heads, tm):
    return pl.BlockSpec((1, n_heads, V_ROWS, tm), lambda b, j: (b, 0, 0, j))


def _l0_proj(h, tabs, w, batch, seq, tm):
    n_j = seq // tm
    row = lambda n: pl.BlockSpec((tm, n), lambda b, j: (b * n_j + j, 0))
    tab = pl.BlockSpec((tm, LANE), lambda b, j: (j, 0))
    ins = [h, w["g_mix"], w["w_in"], w["q_a_g"], w["kv_a_g"], w["w_uq"], w["w_uk"], w["w_uv"],
           w["q_g"], w["k_g"], *tabs]
    in_specs = [row(D_MODEL)] + [_resident(a.shape) for a in ins[1:10]] + [tab] * 3
    out_shape = [jax.ShapeDtypeStruct((batch, MLA_HEADS, seq, MLA_HEAD_PAD), BF16),
                 jax.ShapeDtypeStruct((batch, MLA_HEADS, seq, MLA_HEAD_PAD), BF16),
                 jax.ShapeDtypeStruct((batch, MLA_HEADS, V_ROWS, seq), BF16),
                 jax.ShapeDtypeStruct((batch * seq, CONV_DIM), F32),
                 jax.ShapeDtypeStruct((batch * seq, CONV_DIM), F32)]
    out_specs = [_head_specs(MLA_HEADS, tm, MLA_HEAD_PAD), _head_specs(MLA_HEADS, tm, MLA_HEAD_PAD),
                 _vt_spec(MLA_HEADS, tm), row(CONV_DIM), row(CONV_DIM)]
    return pl.pallas_call(_l0_proj_kernel, grid=(batch, n_j), in_specs=in_specs,
                          out_specs=out_specs, out_shape=out_shape,
                          compiler_params=_params(2), name="l0_proj")(*ins)


def _l1_proj_kernel(h_ref, gmix_ref, wqkv_ref, qg_ref, kg_ref, cos_ref, sa_ref, sb_ref,
                    q_ref, k_ref, vt_ref):
    hn = _rms(h_ref[...], gmix_ref[...]).astype(BF16)
    z = jnp.dot(hn, wqkv_ref[...], preferred_element_type=F32)
    cos, sa, sb = cos_ref[...], sa_ref[...], sb_ref[...]
    q_scale = GQA_HEAD_DIM ** -0.5 * LOG2E
    ones = _ones_rows(z.shape[0])
    for h in range(GQA_Q_HEADS):
        x = z[:, h * LANE:(h + 1) * LANE]
        q_ref[0, h] = (_rope(_rms(x, qg_ref[...]), cos, sa, sb) * q_scale).astype(BF16)
    for h in range(GQA_KV_HEADS):
        c = (GQA_Q_HEADS + h) * LANE
        k_ref[0, h] = _rope(_rms(z[:, c:c + LANE], kg_ref[...]), cos, sa, sb).astype(BF16)
        c = (GQA_Q_HEADS + GQA_KV_HEADS + h) * LANE
        vt_ref[0, h, :GQA_HEAD_DIM, :] = z[:, c:c + LANE].T.astype(BF16)
        vt_ref[0, h, GQA_HEAD_DIM:, :] = ones


def _l1_proj(h, tabs, w, batch, seq, tm):
    n_j = seq // tm
    row = lambda n: pl.BlockSpec((tm, n), lambda b, j: (b * n_j + j, 0))
    tab = pl.BlockSpec((tm, LANE), lambda b, j: (j, 0))
    ins = [h, w["g_mix"], w["w_qkv"], w["q_g"], w["k_g"], *tabs]
    in_specs = [row(D_MODEL)] + [_resident(a.shape) for a in ins[1:5]] + [tab] * 3
    out_shape = [jax.ShapeDtypeStruct((batch, GQA_Q_HEADS, seq, GQA_HEAD_DIM), BF16),
                 jax.ShapeDtypeStruct((batch, GQA_KV_HEADS, seq, GQA_HEAD_DIM), BF16),
                 jax.ShapeDtypeStruct((batch, GQA_KV_HEADS, V_ROWS, seq), BF16)]
    out_specs = [_head_specs(GQA_Q_HEADS, tm, GQA_HEAD_DIM), _head_specs(GQA_KV_HEADS, tm, GQA_HEAD_DIM),
                 _vt_spec(GQA_KV_HEADS, tm)]
    return pl.pallas_call(_l1_proj_kernel, grid=(batch, n_j), in_specs=in_specs,
                          out_specs=out_specs, out_shape=out_shape,
                          compiler_params=_params(2), name="l1_proj")(*ins)


def _meta_heads(x, batch):
    _, n_heads, _, d = x.shape
    return x.reshape(n_heads, batch, N_META, d).transpose(1, 0, 2, 3)


def _meta_vt(vt, batch):
    n_heads = vt.shape[1]
    return vt.reshape(n_heads, V_ROWS, batch, N_META).transpose(2, 0, 1, 3)


def _attn_kernel(q_ref, k_ref, vt_ref, qm_ref, km_ref, vtm_ref, o_ref, om_ref, sa_ref, sb_ref,
                 *, tq, kc):
    seq, dk = k_ref.shape[2], k_ref.shape[3]
    dv = o_ref.shape[2]
    group = qm_ref.shape[1]
    n_tiles = q_ref.shape[1] // tq
    nt = (((1,), (1,)), ((), ()))

    def score_chunk(q, s_nxt, k_rows, lo, n):
        s = lax.dot_general(k_rows, q, nt, preferred_element_type=F32)
        s_nxt[lo:lo + n, :] = s
        return jnp.max(s.reshape(n // SUBLANE, SUBLANE, tq), axis=0)

    def value_chunk(m, s_cur, vt_cols, lo, n):
        p = jnp.exp2(s_cur[lo:lo + n, :] - m).astype(BF16)
        return jnp.dot(vt_cols, p, preferred_element_type=F32)

    def tile(q_next, s_nxt, m_cur, s_cur):
        m8 = jnp.full((SUBLANE, tq), -jnp.inf, F32)
        acc = jnp.zeros((V_ROWS, tq), F32)
        for lo in range(0, seq, kc):
            if q_next is not None:
                m8 = jnp.maximum(m8, score_chunk(q_next, s_nxt, k_ref[0, 0, lo:lo + kc, :], lo, kc))
            if m_cur is not None:
                acc = acc + value_chunk(m_cur, s_cur, vt_ref[0, 0, :, lo:lo + kc], lo, kc)
        if q_next is not None:
            m8 = jnp.maximum(m8, score_chunk(q_next, s_nxt, km_ref[0, 0], seq, N_META))
        if m_cur is not None:
            acc = acc + value_chunk(m_cur, s_cur, vtm_ref[0, 0], seq, N_META)
        out = None if m_cur is None else (acc[:dv] * (1.0 / acc[dv:dv + 1])).T.astype(o_ref.dtype)
        m_next = None if q_next is None else jnp.max(m8, axis=0, keepdims=True)
        return out, m_next

    def q_tile(t):
        return q_ref[0, pl.ds(pl.multiple_of(t * tq, tq), tq), :]

    def store_tile(t, out):
        o_ref[0, pl.ds(pl.multiple_of(t * tq, tq), tq), :] = out

    qm = qm_ref[0].reshape(group * N_META, dk)
    qm = jnp.concatenate([qm, jnp.zeros((tq - group * N_META, dk), qm.dtype)], axis=0)
    _, m_meta = tile(qm, sa_ref, None, None)
    out, m0 = tile(q_tile(0), sb_ref, m_meta, sa_ref)
    for g in range(group):
        om_ref[0, g] = out[g * N_META:(g + 1) * N_META]

    def body(i, m):
        out, m = tile(q_tile(2 * i + 1), sa_ref, m, sb_ref)
        store_tile(2 * i, out)
        out, m = tile(q_tile(2 * i + 2), sb_ref, m, sa_ref)
        store_tile(2 * i + 1, out)
        return m

    m = lax.fori_loop(0, n_tiles // 2 - 1, body, m0)
    out, m = tile(q_tile(n_tiles - 1), sa_ref, m, sb_ref)
    store_tile(n_tiles - 2, out)
    out, _ = tile(None, None, m, sa_ref)
    store_tile(n_tiles - 1, out)


def _attention(q, k, vt, qm, km, vtm, *, group, tq, kc):
    batch, n_q, seq, dk = q.shape
    n_kv = k.shape[1]
    dv = V_ROWS - BF16_SUBLANES
    q = q.reshape(batch, n_q * seq, dk)
    in_specs = [pl.BlockSpec((1, group * seq, dk), lambda b, h: (b, h, 0)),
                pl.BlockSpec((1, 1, seq, dk), lambda b, h: (b, h, 0, 0)),
                pl.BlockSpec((1, 1, V_ROWS, seq), lambda b, h: (b, h, 0, 0)),
                pl.BlockSpec((1, group, N_META, dk), lambda b, h: (b, h, 0, 0)),
                pl.BlockSpec((1, 1, N_META, dk), lambda b, h: (b, h, 0, 0)),
                pl.BlockSpec((1, 1, V_ROWS, N_META), lambda b, h: (b, h, 0, 0))]
    out_shape = [jax.ShapeDtypeStruct((batch, n_q * seq, dv), BF16),
                 jax.ShapeDtypeStruct((batch, n_q, N_META, dv), BF16)]
    out_specs = [pl.BlockSpec((1, group * seq, dv), lambda b, h: (b, h, 0)),
                 pl.BlockSpec((1, group, N_META, dv), lambda b, h: (b, h, 0, 0))]
    o, om = pl.pallas_call(
        functools.partial(_attn_kernel, tq=tq, kc=kc), grid=(batch, n_kv),
        in_specs=in_specs, out_specs=out_specs, out_shape=out_shape,
        scratch_shapes=[pltpu.VMEM((seq + N_META, tq), F32)] * 2,
        compiler_params=_params(2), name="attention")(q, k, vt, qm, km, vtm)
    return o.reshape(batch, n_q, seq, dv), om


def _ffn_tail(h1, gffn_ref, w1_ref, w3_ref, w2_ref):
    hn = _rms(h1, gffn_ref[...]).astype(BF16)
    out = h1
    for c in range(0, FFN_HIDDEN, FFN_CHUNK):
        a = jnp.dot(hn, w1_ref[:, c:c + FFN_CHUNK], preferred_element_type=F32)
        b = jnp.dot(hn, w3_ref[:, c:c + FFN_CHUNK], preferred_element_type=F32)
        g = (a * jax.nn.sigmoid(a) * b).astype(BF16)
        out = out + jnp.dot(g, w2_ref[c:c + FFN_CHUNK, :], preferred_element_type=F32)
    return out


def _heads_to_lanes(at_ref):
    return jnp.concatenate([at_ref[0, h] for h in range(at_ref.shape[1])], axis=1)


def _conv_gate(gb, cu, left_edge, right_edge, first, last, cw_ref):
    n = cu.shape[0]
    left = jnp.where(first, left_edge, pltpu.roll(cu, 1, 0))
    right = jnp.where(last, right_edge, pltpu.roll(cu, n - 1, 0))
    conv = left * cw_ref[0:1, :] + cu * cw_ref[1:2, :] + right * cw_ref[2:3, :]
    return gb * conv


def _l0_out_seq_kernel(h_ref, at_ref, gb_ref, cu_ref, prev_ref, next_ref, cum_ref, cw_ref,
                       woa_ref, woc_ref, gffn_ref, w1_ref, w3_ref, w2_ref, o_ref, *, tiles_per_seq):
    j = pl.program_id(0) % tiles_per_seq
    cu = cu_ref[...]
    tm = cu.shape[0]
    row = lax.broadcasted_iota(jnp.int32, (tm, 1), 0)
    left_edge = jnp.where(j == 0, cum_ref[N_META - 1:N_META, :], prev_ref[SUBLANE - 1:SUBLANE, :])
    right_edge = jnp.where(j == tiles_per_seq - 1, 0.0, next_ref[0:1, :])
    y = _conv_gate(gb_ref[...], cu, left_edge, right_edge, row == 0, row == tm - 1, cw_ref)
    mix = (jnp.dot(_heads_to_lanes(at_ref), woa_ref[...], preferred_element_type=F32)
           + jnp.dot(y.astype(BF16), woc_ref[...], preferred_element_type=F32))
    o_ref[...] = _ffn_tail(h_ref[...] + mix, gffn_ref, w1_ref, w3_ref, w2_ref)


def _l0_out_meta_kernel(h_ref, at_ref, gb_ref, cu_ref, nxt_ref, cw_ref,
                        woa_ref, woc_ref, gffn_ref, w1_ref, w3_ref, w2_ref, o_ref):
    cu = cu_ref[...]
    pos = lax.broadcasted_iota(jnp.int32, (cu.shape[0], 1), 0) % N_META
    y = _conv_gate(gb_ref[...], cu, 0.0, nxt_ref[...], pos == 0, pos == N_META - 1, cw_ref)
    mix = (jnp.dot(_heads_to_lanes(at_ref), woa_ref[...], preferred_element_type=F32)
           + jnp.dot(y.astype(BF16), woc_ref[...], preferred_element_type=F32))
    o_ref[...] = _ffn_tail(h_ref[...] + mix, gffn_ref, w1_ref, w3_ref, w2_ref)


def _l1_out_kernel(h_ref, at_ref, wo_ref, gffn_ref, w1_ref, w3_ref, w2_ref, o_ref):
    mix = jnp.dot(_heads_to_lanes(at_ref), wo_ref[...], preferred_element_type=F32)
    o_ref[...] = _ffn_tail(h_ref[...] + mix, gffn_ref, w1_ref, w3_ref, w2_ref)


def _ffn_weights(w):
    return [w["g_ffn"], w["w1"], w["w3"], w["w2"]]


def _attn_in_spec(attn, tm, tiles_per_seq):
    _, n_heads, _, dv = attn.shape
    return pl.BlockSpec((1, n_heads, tm, dv), lambda i: (i // tiles_per_seq, 0, i % tiles_per_seq, 0))


def _l0_out_seq(h, attn, gb, cu, cu_meta, w, *, seq, tm):
    rows = h.shape[0]
    tiles_per_seq = seq // tm
    sub = tm // SUBLANE
    n_sub = rows // SUBLANE
    row = lambda n: pl.BlockSpec((tm, n), lambda i: (i, 0))
    weights = [w["conv_w"], w["w_out_a"], w["w_out_c"]] + _ffn_weights(w)
    in_specs = [row(D_MODEL), _attn_in_spec(attn, tm, tiles_per_seq), row(CONV_DIM), row(CONV_DIM),
                pl.BlockSpec((SUBLANE, CONV_DIM), lambda i: (jnp.maximum(i * sub - 1, 0), 0)),
                pl.BlockSpec((SUBLANE, CONV_DIM), lambda i: (jnp.minimum((i + 1) * sub, n_sub - 1), 0)),
                pl.BlockSpec((N_META, CONV_DIM), lambda i: (i // tiles_per_seq, 0))]
    in_specs += [_resident(a.shape) for a in weights]
    return pl.pallas_call(functools.partial(_l0_out_seq_kernel, tiles_per_seq=tiles_per_seq),
                          grid=(rows // tm,), in_specs=in_specs, out_specs=row(D_MODEL),
                          out_shape=jax.ShapeDtypeStruct((rows, D_MODEL), F32),
                          compiler_params=_params(1), name="l0_out_seq")(
                              h, attn, gb, cu, cu, cu, cu_meta, *weights)


def _l0_out_meta(h, attn, gb, cu, nxt, w):
    rows = h.shape[0]
    weights = [w["conv_w"], w["w_out_a"], w["w_out_c"]] + _ffn_weights(w)
    ins = [h, attn, gb, cu, nxt, *weights]
    return pl.pallas_call(_l0_out_meta_kernel, grid=(1,),
                          in_specs=[_resident(a.shape) for a in ins],
                          out_specs=pl.BlockSpec((rows, D_MODEL), lambda i: (0, 0)),
                          out_shape=jax.ShapeDtypeStruct((rows, D_MODEL), F32),
                          compiler_params=_params(1), name="l0_out_meta")(*ins)


def _l1_out(h, attn, w, *, seq, tm):
    rows = h.shape[0]
    tiles_per_seq = seq // tm
    row = lambda n: pl.BlockSpec((tm, n), lambda i: (i, 0))
    weights = [w["w_out"]] + _ffn_weights(w)
    return pl.pallas_call(_l1_out_kernel, grid=(rows // tm,),
                          in_specs=[row(D_MODEL), _attn_in_spec(attn, tm, tiles_per_seq)]
                          + [_resident(a.shape) for a in weights],
                          out_specs=row(D_MODEL),
                          out_shape=jax.ShapeDtypeStruct((rows, D_MODEL), F32),
                          compiler_params=_params(1), name="l1_out")(h, attn, *weights)


def _meta_attn_rows(om):
    batch, n_heads, _, dv = om.shape
    return om.transpose(1, 0, 2, 3).reshape(1, n_heads, batch * N_META, dv)


def _rope_freqs(dim):
    return ROPE_THETA ** (-jnp.arange(0, dim, 2, dtype=F32) / dim)


def _rope_tables(ang_a, ang_b):
    z = jnp.zeros_like(ang_a)
    ca, sa, cb, sb = jnp.cos(ang_a), jnp.sin(ang_a), jnp.cos(ang_b), jnp.sin(ang_b)
    return (jnp.concatenate([ca, ca, cb, cb], axis=1),
            jnp.concatenate([z, sa, z, sb], axis=1),
            jnp.concatenate([-sa, z, -sb, z], axis=1))


def _line_tables(seq):
    pos = jnp.arange(N_META + seq, dtype=F32)
    ang = pos[:, None] * _rope_freqs(MLA_ROPE)[None, :]
    cos, sa, sb = _rope_tables(ang, jnp.zeros_like(ang))
    keep = (jnp.arange(LANE) < MLA_ROPE).astype(F32)[None, :]
    return tuple(t * keep for t in (cos, sa, sb))


def _axial_tables(seq):
    rows = seq // GRID_W
    f = _rope_freqs(AXIAL_DIM)
    row_idx = jnp.repeat(jnp.arange(rows, dtype=F32), GRID_W)
    col_idx = jnp.tile(jnp.arange(GRID_W, dtype=F32), rows)
    meta = jnp.zeros((N_META, AXIAL_DIM // 2), F32)
    ang_r = jnp.concatenate([meta, row_idx[:, None] * f[None, :]], axis=0)
    ang_c = jnp.concatenate([meta, col_idx[:, None] * f[None, :]], axis=0)
    return _rope_tables(ang_r, ang_c)


def _split_tables(tabs, batch):
    seq_t = tuple(t[N_META:] for t in tabs)
    meta_t = tuple(jnp.tile(t[:N_META], (batch, 1)) for t in tabs)
    return seq_t, meta_t


def _pad_gain(g):
    return jnp.concatenate([g, jnp.zeros((MLA_HEAD_PAD - MLA_QK,), g.dtype)])[None, :]


def _prep_even(i, l, p):
    w_in = p["mla_w_in"][i]
    cut = MLA_Q_RANK + MLA_KV_RANK + MLA_ROPE
    w_in = jnp.concatenate([w_in[:, :cut], jnp.zeros((D_MODEL, LANE - MLA_ROPE), F32), w_in[:, cut:]], axis=1)
    w_uq = p["mla_w_uq"][i].reshape(MLA_Q_RANK, MLA_HEADS, MLA_QK)
    w_uq = jnp.pad(w_uq, ((0, 0), (0, 0), (0, MLA_HEAD_PAD - MLA_QK))).reshape(MLA_Q_RANK, -1)
    w_ukv = p["mla_w_ukv"][i].reshape(MLA_KV_RANK, MLA_HEADS, MLA_NOPE + MLA_V)
    w_out = p["even_w_out"][i]
    return {
        "g_mix": p["mix_norm_g"][l][None, :], "w_in": w_in.astype(BF16),
        "q_a_g": p["mla_q_a_g"][i][None, :], "kv_a_g": p["mla_kv_a_g"][i][None, :],
        "w_uq": w_uq.astype(BF16),
        "w_uk": w_ukv[:, :, :MLA_NOPE].reshape(MLA_KV_RANK, -1).astype(BF16),
        "w_uv": w_ukv[:, :, MLA_NOPE:].reshape(MLA_KV_RANK, -1).astype(BF16),
        "q_g": _pad_gain(p["mla_q_g"][i]), "k_g": _pad_gain(p["mla_k_g"][i]),
        "conv_w": p["conv_w"][i],
        "w_out_a": w_out[:MLA_HEADS * MLA_V].astype(BF16), "w_out_c": w_out[MLA_HEADS * MLA_V:].astype(BF16),
        "g_ffn": p["ffn_norm_g"][l][None, :], "w1": p["ffn_w1"][l].astype(BF16),
        "w3": p["ffn_w3"][l].astype(BF16), "w2": p["ffn_w2"][l].astype(BF16),
    }


def _prep_odd(i, l, p):
    return {
        "g_mix": p["mix_norm_g"][l][None, :], "w_qkv": p["gqa_w_qkv"][i].astype(BF16),
        "q_g": p["gqa_q_g"][i][None, :], "k_g": p["gqa_k_g"][i][None, :],
        "w_out": p["odd_w_out"][i].astype(BF16),
        "g_ffn": p["ffn_norm_g"][l][None, :], "w1": p["ffn_w1"][l].astype(BF16),
        "w3": p["ffn_w3"][l].astype(BF16), "w2": p["ffn_w2"][l].astype(BF16),
    }


def _trunk(x, meta_tokens, layers, line_tabs, axial_tabs):
    batch, seq, _ = x.shape
    tm = min(ROW_TILE, seq)
    tq = min(Q_TILE, seq)
    kc = min(KEY_CHUNK, seq)
    hs = x.reshape(batch * seq, D_MODEL)
    hm = jnp.tile(meta_tokens.astype(x.dtype), (batch, 1))
    rows_m = batch * N_META
    for kind, w in layers:
        if kind == "even":
            tabs_s, tabs_m = _split_tables(line_tabs, batch)
            q, k, vt, gb, cu = _l0_proj(hs, tabs_s, w, batch, seq, tm)
            qm, km, vtm, gbm, cum = _l0_proj(hm, tabs_m, w, 1, rows_m, rows_m)
            o, om = _attention(q, k, vt, _meta_heads(qm, batch), _meta_heads(km, batch),
                               _meta_vt(vtm, batch), group=1, tq=tq, kc=kc)
            first = cu.reshape(batch, seq, CONV_DIM)[:, 0]
            nxt = jnp.zeros((batch, N_META, CONV_DIM), F32).at[:, N_META - 1].set(first)
            hs_new = _l0_out_seq(hs, o, gb, cu, cum, w, seq=seq, tm=tm)
            hm = _l0_out_meta(hm, _meta_attn_rows(om), gbm, cum, nxt.reshape(rows_m, CONV_DIM), w)
            hs = hs_new
        else:
            tabs_s, tabs_m = _split_tables(axial_tabs, batch)
            q, k, vt = _l1_proj(hs, tabs_s, w, batch, seq, tm)
            qm, km, vtm = _l1_proj(hm, tabs_m, w, 1, rows_m, rows_m)
            o, om = _attention(q, k, vt, _meta_heads(qm, batch), _meta_heads(km, batch),
                               _meta_vt(vtm, batch), group=GQA_GROUP, tq=tq, kc=kc)
            hs = _l1_out(hs, o, w, seq=seq, tm=tm)
            hm = _l1_out(hm, _meta_attn_rows(om), w, seq=rows_m, tm=rows_m)
    return hs.reshape(batch, seq, D_MODEL)


def kernel(x_prompt, x_sample, meta_tokens, mix_norm_g, ffn_norm_g, mla_w_in, mla_q_a_g, mla_kv_a_g, mla_w_uq, mla_w_ukv, mla_q_g, mla_k_g, conv_w, even_w_out, gqa_w_qkv, gqa_q_g, gqa_k_g, odd_w_out, ffn_w1, ffn_w3, ffn_w2):
    p = dict(mix_norm_g=mix_norm_g, ffn_norm_g=ffn_norm_g, mla_w_in=mla_w_in, mla_q_a_g=mla_q_a_g,
             mla_kv_a_g=mla_kv_a_g, mla_w_uq=mla_w_uq, mla_w_ukv=mla_w_ukv, mla_q_g=mla_q_g,
             mla_k_g=mla_k_g, conv_w=conv_w, even_w_out=even_w_out, gqa_w_qkv=gqa_w_qkv,
             gqa_q_g=gqa_q_g, gqa_k_g=gqa_k_g, odd_w_out=odd_w_out, ffn_w1=ffn_w1, ffn_w3=ffn_w3,
             ffn_w2=ffn_w2)
    depth = mix_norm_g.shape[0]
    layers = [("even", _prep_even(l // 2, l, p)) if l % 2 == 0 else ("odd", _prep_odd(l // 2, l, p))
              for l in range(depth)]
    outs = []
    for x in (x_prompt, x_sample):
        seq = x.shape[1]
        outs.append(_trunk(x, meta_tokens, layers, _line_tables(seq), _axial_tables(seq)))
    return tuple(outs)
```

```python
import functools
import math

import jax
import jax.numpy as jnp
from jax import lax
from jax.experimental import pallas as pl
from jax.experimental.pallas import tpu as pltpu

F32 = jnp.float32
BF16 = jnp.bfloat16

D_MODEL = 1024
N_META = 16
GRID_W = 64
RMS_EPS = 1e-6
ROPE_THETA = 10000.0

MLA_HEADS = 4
MLA_Q_RANK = 384
MLA_KV_RANK = 256
MLA_NOPE = 128
MLA_ROPE = 64
MLA_V = 128
MLA_QK = MLA_NOPE + MLA_ROPE
CONV_DIM = D_MODEL // 2
CONV_WIDTH = 3

GQA_Q_HEADS = 8
GQA_KV_HEADS = 2
GQA_GROUP = GQA_Q_HEADS // GQA_KV_HEADS
GQA_HEAD_DIM = 128
AXIAL_DIM = GQA_HEAD_DIM // 2

FFN_HIDDEN = ((8 * D_MODEL + 3 * 256 - 1) // (3 * 256)) * 256

LANE = 128
SUBLANE = 8
BF16_SUBLANES = 16
MLA_HEAD_PAD = 2 * LANE
ROPE_HALF = 32
V_ROWS = MLA_V + BF16_SUBLANES
LOG2E = math.log2(math.e)
VMEM_LIMIT = 56 * 1024 * 1024

ROW_TILE = 512
Q_TILE = 256
KEY_CHUNK = 256
MXU_WIDTH = 256
_FFN_TILES = FFN_HIDDEN // MXU_WIDTH
FFN_SPLIT = (_FFN_TILES + 1) // 2 * MXU_WIDTH
FFN_CHUNKS = ((0, FFN_SPLIT), (FFN_SPLIT, FFN_HIDDEN))


def _params(n_axes):
    return pltpu.CompilerParams(dimension_semantics=("arbitrary",) * n_axes,
                                vmem_limit_bytes=VMEM_LIMIT)


def _resident(shape):
    nd = len(shape)
    return pl.BlockSpec(shape, lambda *_: (0,) * nd, pipeline_mode=pl.Buffered(1))


def _rms(x, g):
    ms = jnp.mean(x * x, axis=-1, keepdims=True)
    return x * lax.rsqrt(ms + RMS_EPS) * g


def _rms_t(x, g_col):
    ms = jnp.mean(x * x, axis=0, keepdims=True)
    return x * lax.rsqrt(ms + RMS_EPS) * g_col


def _rope_t(x, cos, sin):
    blocks = []
    for lo in range(0, x.shape[0], 2 * ROPE_HALF):
        blocks += [x[lo + ROPE_HALF:lo + 2 * ROPE_HALF], x[lo:lo + ROPE_HALF]]
    return x * cos + jnp.concatenate(blocks, axis=0) * sin


def _ones_rows(n):
    row = lax.broadcasted_iota(jnp.int32, (BF16_SUBLANES, n), 0)
    return jnp.where(row == 0, 1.0, 0.0).astype(BF16)


def _store_q_tiles(qt_ref, h, x):
    tq = qt_ref.shape[-1]
    for j in range(x.shape[1] // tq):
        qt_ref[0, h, j] = x[:, j * tq:(j + 1) * tq]


_NT = (((1,), (1,)), ((), ()))


def _l0_proj_kernel(h_ref, gmix_ref, wmt_ref, wconv_ref, qag_ref, kvag_ref, wuqt_ref, wukt_ref,
                    wuvt_ref, qg_ref, kg_ref, cos_ref, sin_ref,
                    qt_ref, k_ref, vt_ref, gb_ref, cu_ref):
    hn = _rms(h_ref[...], gmix_ref[...]).astype(BF16)
    tm = hn.shape[0]
    zc = jnp.dot(hn, wconv_ref[...], preferred_element_type=F32)
    gb_ref[...] = zc[:, :CONV_DIM]
    cu_ref[...] = zc[:, CONV_DIM:2 * CONV_DIM] * zc[:, 2 * CONV_DIM:]

    zt = lax.dot_general(wmt_ref[...], hn, _NT, preferred_element_type=F32)
    cq = _rms_t(zt[:MLA_Q_RANK], qag_ref[...]).astype(BF16)
    ckv = _rms_t(zt[MLA_Q_RANK:MLA_Q_RANK + MLA_KV_RANK], kvag_ref[...]).astype(BF16)
    k_r = zt[MLA_Q_RANK + MLA_KV_RANK:]
    qt = jnp.dot(wuqt_ref[...], cq, preferred_element_type=F32)
    knt = jnp.dot(wukt_ref[...], ckv, preferred_element_type=F32)
    vt = jnp.dot(wuvt_ref[...], ckv, preferred_element_type=F32)

    cos, sin = cos_ref[...], sin_ref[...]
    qg, kg = qg_ref[...], kg_ref[...]
    q_scale = MLA_QK ** -0.5 * LOG2E
    kr_sq = jnp.sum(k_r * k_r, axis=0, keepdims=True)
    kr_rot = _rope_t(k_r * kg[MLA_NOPE:], cos, sin)
    zpad = jnp.zeros((MLA_HEAD_PAD - MLA_QK, tm), F32)
    ones = _ones_rows(tm)
    for h in range(MLA_HEADS):
        x = qt[h * MLA_QK:(h + 1) * MLA_QK]
        r = lax.rsqrt(jnp.mean(x * x, axis=0, keepdims=True) + RMS_EPS) * q_scale
        xn = x * r * qg
        qh = jnp.concatenate([xn[:MLA_NOPE], _rope_t(xn[MLA_NOPE:], cos, sin), zpad], axis=0)
        _store_q_tiles(qt_ref, h, qh.astype(BF16))
        kh = knt[h * MLA_NOPE:(h + 1) * MLA_NOPE]
        ms = (jnp.sum(kh * kh, axis=0, keepdims=True) + kr_sq) / MLA_QK
        r = lax.rsqrt(ms + RMS_EPS)
        kt = jnp.concatenate([kh * r * kg[:MLA_NOPE], kr_rot * r, zpad], axis=0)
        k_ref[0, h] = kt.T.astype(BF16)
        vt_ref[0, h, :MLA_V, :] = vt[h * MLA_V:(h + 1) * MLA_V].astype(BF16)
        vt_ref[0, h, MLA_V:, :] = ones


def _qt_spec(n_heads, tm, dk, tq):
    return pl.BlockSpec((1, n_heads, tm // tq, dk, tq), lambda b, j: (b, 0, j, 0, 0))


def _head_spec(n_heads, tm, dk):
    return pl.BlockSpec((1, n_heads, tm, dk), lambda b, j: (b, 0, j, 0))


def _vt_spec(n_heads, tm):
    return pl.BlockSpec((1, n_heads, V_ROWS, tm), lambda b, j: (b, 0, 0, j))


def _l0_proj(h, tabs, w, batch, seq, tm, tq):
    n_j = seq // tm
    row = lambda n: pl.BlockSpec((tm, n), lambda b, j: (b * n_j + j, 0))
    tab = pl.BlockSpec((tabs[0].shape[0], tm), lambda b, j: (0, j))
    ins = [h, w["g_mix"], w["w_mla_t"], w["w_conv"], w["q_a_g"], w["kv_a_g"], w["w_uq_t"],
           w["w_uk_t"], w["w_uv_t"], w["q_g"], w["k_g"], *tabs]
    in_specs = [row(D_MODEL)] + [_resident(a.shape) for a in ins[1:11]] + [tab] * 2
    out_shape = [jax.ShapeDtypeStruct((batch, MLA_HEADS, seq // tq, MLA_HEAD_PAD, tq), BF16),
                 jax.ShapeDtypeStruct((batch, MLA_HEADS, seq, MLA_HEAD_PAD), BF16),
                 jax.ShapeDtypeStruct((batch, MLA_HEADS, V_ROWS, seq), BF16),
                 jax.ShapeDtypeStruct((batch * seq, CONV_DIM), F32),
                 jax.ShapeDtypeStruct((batch * seq, CONV_DIM), F32)]
    out_specs = [_qt_spec(MLA_HEADS, tm, MLA_HEAD_PAD, tq), _head_spec(MLA_HEADS, tm, MLA_HEAD_PAD),
                 _vt_spec(MLA_HEADS, tm), row(CONV_DIM), row(CONV_DIM)]
    return pl.pallas_call(_l0_proj_kernel, grid=(batch, n_j), in_specs=in_specs,
                          out_specs=out_specs, out_shape=out_shape,
                          compiler_params=_params(2), name="l0_proj")(*ins)


def _l1_proj_kernel(h_ref, gmix_ref, wt_ref, qg_ref, kg_ref, cos_ref, sin_ref, qt_ref, k_ref, vt_ref):
    hn = _rms(h_ref[...], gmix_ref[...]).astype(BF16)
    zt = lax.dot_general(wt_ref[...], hn, _NT, preferred_element_type=F32)
    cos, sin = cos_ref[...], sin_ref[...]
    q_scale = GQA_HEAD_DIM ** -0.5 * LOG2E
    ones = _ones_rows(hn.shape[0])
    d = GQA_HEAD_DIM
    for h in range(GQA_Q_HEADS):
        x = _rope_t(_rms_t(zt[h * d:(h + 1) * d], qg_ref[...]), cos, sin) * q_scale
        _store_q_tiles(qt_ref, h, x.astype(BF16))
    for h in range(GQA_KV_HEADS):
        c = (GQA_Q_HEADS + h) * d
        k_ref[0, h] = _rope_t(_rms_t(zt[c:c + d], kg_ref[...]), cos, sin).T.astype(BF16)
        c = (GQA_Q_HEADS + GQA_KV_HEADS + h) * d
        vt_ref[0, h, :d, :] = zt[c:c + d].astype(BF16)
        vt_ref[0, h, d:, :] = ones


def _l1_proj(h, tabs, w, batch, seq, tm, tq):
    n_j = seq // tm
    row = lambda n: pl.BlockSpec((tm, n), lambda b, j: (b * n_j + j, 0))
    tab = pl.BlockSpec((tabs[0].shape[0], tm), lambda b, j: (0, j))
    ins = [h, w["g_mix"], w["w_qkv_t"], w["q_g"], w["k_g"], *tabs]
    in_specs = [row(D_MODEL)] + [_resident(a.shape) for a in ins[1:5]] + [tab] * 2
    out_shape = [jax.ShapeDtypeStruct((batch, GQA_Q_HEADS, seq // tq, GQA_HEAD_DIM, tq), BF16),
                 jax.ShapeDtypeStruct((batch, GQA_KV_HEADS, seq, GQA_HEAD_DIM), BF16),
                 jax.ShapeDtypeStruct((batch, GQA_KV_HEADS, V_ROWS, seq), BF16)]
    out_specs = [_qt_spec(GQA_Q_HEADS, tm, GQA_HEAD_DIM, tq), _head_spec(GQA_KV_HEADS, tm, GQA_HEAD_DIM),
                 _vt_spec(GQA_KV_HEADS, tm)]
    return pl.pallas_call(_l1_proj_kernel, grid=(batch, n_j), in_specs=in_specs,
                          out_specs=out_specs, out_shape=out_shape,
                          compiler_params=_params(2), name="l1_proj")(*ins)


def _meta_heads(x, batch):
    _, n_heads, _, d = x.shape
    return x.reshape(n_heads, batch, N_META, d).transpose(1, 0, 2, 3)


def _meta_vt(vt, batch):
    n_heads = vt.shape[1]
    return vt.reshape(n_heads, V_ROWS, batch, N_META).transpose(2, 0, 1, 3)


def _meta_q_tiles(qt, batch, group, tq):
    n_q, dk = qt.shape[1], qt.shape[3]
    x = qt.reshape(n_q // group, group, dk, batch, N_META).transpose(3, 0, 2, 1, 4)
    x = x.reshape(batch, n_q // group, dk, group * N_META)
    return jnp.pad(x, ((0, 0), (0, 0), (0, 0), (0, tq - group * N_META)))


def _attn_kernel(q_ref, k_ref, vt_ref, qm_ref, km_ref, vtm_ref, o_ref, om_ref, sa_ref, sb_ref, *, kc):
    seq = k_ref.shape[2]
    tq = q_ref.shape[3]
    dv = o_ref.shape[2]
    group = om_ref.shape[1]
    n_tiles = q_ref.shape[1]

    def score_chunk(qt, s_nxt, k_rows, lo, n):
        s = jnp.dot(k_rows, qt, preferred_element_type=F32)
        s_nxt[lo:lo + n, :] = s
        return jnp.max(s.reshape(n // SUBLANE, SUBLANE, tq), axis=0)

    def value_chunk(m, s_cur, vt_cols, lo, n):
        p = jnp.exp2(s_cur[lo:lo + n, :] - m).astype(BF16)
        return jnp.dot(vt_cols, p, preferred_element_type=F32)

    def tile(q_next, s_nxt, m_cur, s_cur):
        m8 = jnp.full((SUBLANE, tq), -jnp.inf, F32)
        acc = jnp.zeros((V_ROWS, tq), F32)
        for lo in range(0, seq, kc):
            if q_next is not None:
                m8 = jnp.maximum(m8, score_chunk(q_next, s_nxt, k_ref[0, 0, lo:lo + kc, :], lo, kc))
            if m_cur is not None:
                acc = acc + value_chunk(m_cur, s_cur, vt_ref[0, 0, :, lo:lo + kc], lo, kc)
        if q_next is not None:
            m8 = jnp.maximum(m8, score_chunk(q_next, s_nxt, km_ref[0, 0], seq, N_META))
        if m_cur is not None:
            acc = acc + value_chunk(m_cur, s_cur, vtm_ref[0, 0], seq, N_META)
        out = None if m_cur is None else (acc[:dv] * (1.0 / acc[dv:dv + 1])).T.astype(o_ref.dtype)
        m_next = None if q_next is None else jnp.max(m8, axis=0, keepdims=True)
        return out, m_next

    def store_tile(t, out):
        o_ref[0, pl.ds(pl.multiple_of(t * tq, tq), tq), :] = out

    _, m_meta = tile(qm_ref[0, 0], sa_ref, None, None)
    out, m0 = tile(q_ref[0, 0], sb_ref, m_meta, sa_ref)
    for g in range(group):
        om_ref[0, g] = out[g * N_META:(g + 1) * N_META]

    def body(i, m):
        out, m = tile(q_ref[0, 2 * i + 1], sa_ref, m, sb_ref)
        store_tile(2 * i, out)
        out, m = tile(q_ref[0, 2 * i + 2], sb_ref, m, sa_ref)
        store_tile(2 * i + 1, out)
        return m

    m = lax.fori_loop(0, n_tiles // 2 - 1, body, m0)
    out, m = tile(q_ref[0, n_tiles - 1], sa_ref, m, sb_ref)
    store_tile(n_tiles - 2, out)
    out, _ = tile(None, None, m, sa_ref)
    store_tile(n_tiles - 1, out)


def _attention(qt, k, vt, qmt, km, vtm, *, group, kc):
    batch, n_q, tiles, dk, tq = qt.shape
    n_kv, seq = k.shape[1], k.shape[2]
    dv = V_ROWS - BF16_SUBLANES
    qt = qt.reshape(batch, n_q * tiles, dk, tq)
    in_specs = [pl.BlockSpec((1, group * tiles, dk, tq), lambda b, h: (b, h, 0, 0)),
                pl.BlockSpec((1, 1, seq, dk), lambda b, h: (b, h, 0, 0)),
                pl.BlockSpec((1, 1, V_ROWS, seq), lambda b, h: (b, h, 0, 0)),
                pl.BlockSpec((1, 1, dk, tq), lambda b, h: (b, h, 0, 0)),
                pl.BlockSpec((1, 1, N_META, dk), lambda b, h: (b, h, 0, 0)),
                pl.BlockSpec((1, 1, V_ROWS, N_META), lambda b, h: (b, h, 0, 0))]
    out_shape = [jax.ShapeDtypeStruct((batch, n_q * seq, dv), BF16),
                 jax.ShapeDtypeStruct((batch, n_q, N_META, dv), BF16)]
    out_specs = [pl.BlockSpec((1, group * seq, dv), lambda b, h: (b, h, 0)),
                 pl.BlockSpec((1, group, N_META, dv), lambda b, h: (b, h, 0, 0))]
    o, om = pl.pallas_call(
        functools.partial(_attn_kernel, kc=kc), grid=(batch, n_kv),
        in_specs=in_specs, out_specs=out_specs, out_shape=out_shape,
        scratch_shapes=[pltpu.VMEM((seq + N_META, tq), F32)] * 2,
        compiler_params=_params(2), name="attention")(qt, k, vt, qmt, km, vtm)
    return o.reshape(batch, n_q, seq, dv), om


def _ffn_tail(h1, gffn_ref, w1_ref, w3_ref, w2_ref):
    hn = _rms(h1, gffn_ref[...]).astype(BF16)
    out = h1
    for lo, hi in FFN_CHUNKS:
        a = jnp.dot(hn, w1_ref[:, lo:hi], preferred_element_type=F32)
        b = jnp.dot(hn, w3_ref[:, lo:hi], preferred_element_type=F32)
        g = (a * jax.nn.sigmoid(a) * b).astype(BF16)
        out = out + jnp.dot(g, w2_ref[lo:hi, :], preferred_element_type=F32)
    return out


def _heads_to_lanes(at_ref):
    return jnp.concatenate([at_ref[0, h] for h in range(at_ref.shape[1])], axis=1)


def _conv_gate(gb, cu, left_edge, right_edge, first, last, cw_ref):
    n = cu.shape[0]
    left = jnp.where(first, left_edge, pltpu.roll(cu, 1, 0))
    right = jnp.where(last, right_edge, pltpu.roll(cu, n - 1, 0))
    conv = left * cw_ref[0:1, :] + cu * cw_ref[1:2, :] + right * cw_ref[2:3, :]
    return gb * conv


def _l0_out_seq_kernel(h_ref, at_ref, gb_ref, cu_ref, prev_ref, next_ref, cum_ref, cw_ref,
                       woa_ref, woc_ref, gffn_ref, w1_ref, w3_ref, w2_ref, o_ref, *, tiles_per_seq):
    j = pl.program_id(0) % tiles_per_seq
    cu = cu_ref[...]
    tm = cu.shape[0]
    row = lax.broadcasted_iota(jnp.int32, (tm, 1), 0)
    left_edge = jnp.where(j == 0, cum_ref[N_META - 1:N_META, :], prev_ref[SUBLANE - 1:SUBLANE, :])
    right_edge = jnp.where(j == tiles_per_seq - 1, 0.0, next_ref[0:1, :])
    y = _conv_gate(gb_ref[...], cu, left_edge, right_edge, row == 0, row == tm - 1, cw_ref)
    mix = (jnp.dot(_heads_to_lanes(at_ref), woa_ref[...], preferred_element_type=F32)
           + jnp.dot(y.astype(BF16), woc_ref[...], preferred_element_type=F32))
    o_ref[...] = _ffn_tail(h_ref[...] + mix, gffn_ref, w1_ref, w3_ref, w2_ref)


def _l0_out_meta_kernel(h_ref, at_ref, gb_ref, cu_ref, nxt_ref, cw_ref,
                        woa_ref, woc_ref, gffn_ref, w1_ref, w3_ref, w2_ref, o_ref):
    cu = cu_ref[...]
    pos = lax.broadcasted_iota(jnp.int32, (cu.shape[0], 1), 0) % N_META
    y = _conv_gate(gb_ref[...], cu, 0.0, nxt_ref[...], pos == 0, pos == N_META - 1, cw_ref)
    mix = (jnp.dot(_heads_to_lanes(at_ref), woa_ref[...], preferred_element_type=F32)
           + jnp.dot(y.astype(BF16), woc_ref[...], preferred_element_type=F32))
    o_ref[...] = _ffn_tail(h_ref[...] + mix, gffn_ref, w1_ref, w3_ref, w2_ref)


def _l1_out_kernel(h_ref, at_ref, wo_ref, gffn_ref, w1_ref, w3_ref, w2_ref, o_ref):
    mix = jnp.dot(_heads_to_lanes(at_ref), wo_ref[...], preferred_element_type=F32)
    o_ref[...] = _ffn_tail(h_ref[...] + mix, gffn_ref, w1_ref, w3_ref, w2_ref)


def _ffn_weights(w):
    return [w["g_ffn"], w["w1"], w["w3"], w["w2"]]


def _attn_in_spec(attn, tm, tiles_per_seq):
    _, n_heads, _, dv = attn.shape
    return pl.BlockSpec((1, n_heads, tm, dv), lambda i: (i // tiles_per_seq, 0, i % tiles_per_seq, 0))


def _l0_out_seq(h, attn, gb, cu, cu_meta, w, *, seq, tm):
    rows = h.shape[0]
    tiles_per_seq = seq // tm
    sub = tm // SUBLANE
    n_sub = rows // SUBLANE
    row = lambda n: pl.BlockSpec((tm, n), lambda i: (i, 0))
    weights = [w["conv_w"], w["w_out_a"], w["w_out_c"]] + _ffn_weights(w)
    in_specs = [row(D_MODEL), _attn_in_spec(attn, tm, tiles_per_seq), row(CONV_DIM), row(CONV_DIM),
                pl.BlockSpec((SUBLANE, CONV_DIM), lambda i: (jnp.maximum(i * sub - 1, 0), 0)),
                pl.BlockSpec((SUBLANE, CONV_DIM), lambda i: (jnp.minimum((i + 1) * sub, n_sub - 1), 0)),
                pl.BlockSpec((N_META, CONV_DIM), lambda i: (i // tiles_per_seq, 0))]
    in_specs += [_resident(a.shape) for a in weights]
    return pl.pallas_call(functools.partial(_l0_out_seq_kernel, tiles_per_seq=tiles_per_seq),
                          grid=(rows // tm,), in_specs=in_specs, out_specs=row(D_MODEL),
                          out_shape=jax.ShapeDtypeStruct((rows, D_MODEL), F32),
                          compiler_params=_params(1), name="l0_out_seq")(
                              h, attn, gb, cu, cu, cu, cu_meta, *weights)


def _l0_out_meta(h, attn, gb, cu, nxt, w):
    rows = h.shape[0]
    weights = [w["conv_w"], w["w_out_a"], w["w_out_c"]] + _ffn_weights(w)
    ins = [h, attn, gb, cu, nxt, *weights]
    return pl.pallas_call(_l0_out_meta_kernel, grid=(1,),
                          in_specs=[_resident(a.shape) for a in ins],
                          out_specs=pl.BlockSpec((rows, D_MODEL), lambda i: (0, 0)),
                          out_shape=jax.ShapeDtypeStruct((rows, D_MODEL), F32),
                          compiler_params=_params(1), name="l0_out_meta")(*ins)


def _l1_out(h, attn, w, *, seq, tm):
    rows = h.shape[0]
    tiles_per_seq = seq // tm
    row = lambda n: pl.BlockSpec((tm, n), lambda i: (i, 0))
    weights = [w["w_out"]] + _ffn_weights(w)
    return pl.pallas_call(_l1_out_kernel, grid=(rows // tm,),
                          in_specs=[row(D_MODEL), _attn_in_spec(attn, tm, tiles_per_seq)]
                          + [_resident(a.shape) for a in weights],
                          out_specs=row(D_MODEL),
                          out_shape=jax.ShapeDtypeStruct((rows, D_MODEL), F32),
                          compiler_params=_params(1), name="l1_out")(h, attn, *weights)


def _meta_attn_rows(om):
    batch, n_heads, _, dv = om.shape
    return om.transpose(1, 0, 2, 3).reshape(1, n_heads, batch * N_META, dv)


def _rope_freqs(dim):
    return ROPE_THETA ** (-jnp.arange(0, dim, 2, dtype=F32) / dim)


def _rope_tables_t(angles):
    cos = jnp.concatenate([jnp.concatenate([jnp.cos(a), jnp.cos(a)], axis=1) for a in angles], axis=1)
    sin = jnp.concatenate([jnp.concatenate([-jnp.sin(a), jnp.sin(a)], axis=1) for a in angles], axis=1)
    return cos.T, sin.T


def _line_tables(seq):
    pos = jnp.arange(N_META + seq, dtype=F32)
    return _rope_tables_t([pos[:, None] * _rope_freqs(MLA_ROPE)[None, :]])


def _axial_tables(seq):
    rows = seq // GRID_W
    f = _rope_freqs(AXIAL_DIM)
    row_idx = jnp.repeat(jnp.arange(rows, dtype=F32), GRID_W)
    col_idx = jnp.tile(jnp.arange(GRID_W, dtype=F32), rows)
    meta = jnp.zeros((N_META, AXIAL_DIM // 2), F32)
    ang_r = jnp.concatenate([meta, row_idx[:, None] * f[None, :]], axis=0)
    ang_c = jnp.concatenate([meta, col_idx[:, None] * f[None, :]], axis=0)
    return _rope_tables_t([ang_r, ang_c])


def _split_tables(tabs, batch):
    seq_t = tuple(t[:, N_META:] for t in tabs)
    meta_t = tuple(jnp.tile(t[:, :N_META], (1, batch)) for t in tabs)
    return seq_t, meta_t


def _col(g):
    return g[:, None]


def _prep_even(i, l, p):
    w_in = p["mla_w_in"][i]
    n_mla = MLA_Q_RANK + MLA_KV_RANK + MLA_ROPE
    w_ukv = p["mla_w_ukv"][i].reshape(MLA_KV_RANK, MLA_HEADS, MLA_NOPE + MLA_V)
    w_out = p["even_w_out"][i]
    return {
        "g_mix": p["mix_norm_g"][l][None, :],
        "w_mla_t": w_in[:, :n_mla].T.astype(BF16), "w_conv": w_in[:, n_mla:].astype(BF16),
        "q_a_g": _col(p["mla_q_a_g"][i]), "kv_a_g": _col(p["mla_kv_a_g"][i]),
        "w_uq_t": p["mla_w_uq"][i].T.astype(BF16),
        "w_uk_t": w_ukv[:, :, :MLA_NOPE].reshape(MLA_KV_RANK, -1).T.astype(BF16),
        "w_uv_t": w_ukv[:, :, MLA_NOPE:].reshape(MLA_KV_RANK, -1).T.astype(BF16),
        "q_g": _col(p["mla_q_g"][i]), "k_g": _col(p["mla_k_g"][i]),
        "conv_w": p["conv_w"][i],
        "w_out_a": w_out[:MLA_HEADS * MLA_V].astype(BF16), "w_out_c": w_out[MLA_HEADS * MLA_V:].astype(BF16),
        "g_ffn": p["ffn_norm_g"][l][None, :], "w1": p["ffn_w1"][l].astype(BF16),
        "w3": p["ffn_w3"][l].astype(BF16), "w2": p["ffn_w2"][l].astype(BF16),
    }


def _prep_odd(i, l, p):
    return {
        "g_mix": p["mix_norm_g"][l][None, :], "w_qkv_t": p["gqa_w_qkv"][i].T.astype(BF16),
        "q_g": _col(p["gqa_q_g"][i]), "k_g": _col(p["gqa_k_g"][i]),
        "w_out": p["odd_w_out"][i].astype(BF16),
        "g_ffn": p["ffn_norm_g"][l][None, :], "w1": p["ffn_w1"][l].astype(BF16),
        "w3": p["ffn_w3"][l].astype(BF16), "w2": p["ffn_w2"][l].astype(BF16),
    }


def _trunk(x, meta_tokens, layers, line_tabs, axial_tabs):
    batch, seq, _ = x.shape
    tm = min(ROW_TILE, seq)
    tq = min(Q_TILE, seq)
    kc = min(KEY_CHUNK, seq)
    hs = x.reshape(batch * seq, D_MODEL)
    hm = jnp.tile(meta_tokens.astype(x.dtype), (batch, 1))
    rows_m = batch * N_META
    for kind, w in layers:
        if kind == "even":
            tabs_s, tabs_m = _split_tables(line_tabs, batch)
            qt, k, vt, gb, cu = _l0_proj(hs, tabs_s, w, batch, seq, tm, tq)
            qmt, km, vtm, gbm, cum = _l0_proj(hm, tabs_m, w, 1, rows_m, rows_m, rows_m)
            o, om = _attention(qt, k, vt, _meta_q_tiles(qmt, batch, 1, tq), _meta_heads(km, batch),
                               _meta_vt(vtm, batch), group=1, kc=kc)
            first = cu.reshape(batch, seq, CONV_DIM)[:, 0]
            nxt = jnp.zeros((batch, N_META, CONV_DIM), F32).at[:, N_META - 1].set(first)
            hs_new = _l0_out_seq(hs, o, gb, cu, cum, w, seq=seq, tm=tm)
            hm = _l0_out_meta(hm, _meta_attn_rows(om), gbm, cum, nxt.reshape(rows_m, CONV_DIM), w)
            hs = hs_new
        else:
            tabs_s, tabs_m = _split_tables(axial_tabs, batch)
            qt, k, vt = _l1_proj(hs, tabs_s, w, batch, seq, tm, tq)
            qmt, km, vtm = _l1_proj(hm, tabs_m, w, 1, rows_m, rows_m, rows_m)
            o, om = _attention(qt, k, vt, _meta_q_tiles(qmt, batch, GQA_GROUP, tq), _meta_heads(km, batch),
                               _meta_vt(vtm, batch), group=GQA_GROUP, kc=kc)
            hs = _l1_out(hs, o, w, seq=seq, tm=tm)
            hm = _l1_out(hm, _meta_attn_rows(om), w, seq=rows_m, tm=rows_m)
    return hs.reshape(batch, seq, D_MODEL)


def kernel(x_prompt, x_sample, meta_tokens, mix_norm_g, ffn_norm_g, mla_w_in, mla_q_a_g, mla_kv_a_g, mla_w_uq, mla_w_ukv, mla_q_g, mla_k_g, conv_w, even_w_out, gqa_w_qkv, gqa_q_g, gqa_k_g, odd_w_out, ffn_w1, ffn_w3, ffn_w2):
    p = dict(mix_norm_g=mix_norm_g, ffn_norm_g=ffn_norm_g, mla_w_in=mla_w_in, mla_q_a_g=mla_q_a_g,
             mla_kv_a_g=mla_kv_a_g, mla_w_uq=mla_w_uq, mla_w_ukv=mla_w_ukv, mla_q_g=mla_q_g,
             mla_k_g=mla_k_g, conv_w=conv_w, even_w_out=even_w_out, gqa_w_qkv=gqa_w_qkv,
             gqa_q_g=gqa_q_g, gqa_k_g=gqa_k_g, odd_w_out=odd_w_out, ffn_w1=ffn_w1, ffn_w3=ffn_w3,
             ffn_w2=ffn_w2)
    depth = mix_norm_g.shape[0]
    layers = [("even", _prep_even(l // 2, l, p)) if l % 2 == 0 else ("odd", _prep_odd(l // 2, l, p))
              for l in range(depth)]
    outs = []
    for x in (x_prompt, x_sample):
        seq = x.shape[1]
        outs.append(_trunk(x, meta_tokens, layers, _line_tables(seq), _axial_tables(seq)))
    return tuple(outs)
```

```python
import functools
import math

import jax
import jax.numpy as jnp
from jax import lax
from jax.experimental import pallas as pl
from jax.experimental.pallas import tpu as pltpu

F32 = jnp.float32
BF16 = jnp.bfloat16

D_MODEL = 1024
N_META = 16
GRID_W = 64
RMS_EPS = 1e-6
ROPE_THETA = 10000.0

MLA_HEADS = 4
MLA_Q_RANK = 384
MLA_KV_RANK = 256
MLA_NOPE = 128
MLA_ROPE = 64
MLA_V = 128
MLA_QK = MLA_NOPE + MLA_ROPE
CONV_DIM = D_MODEL // 2
CONV_WIDTH = 3

GQA_Q_HEADS = 8
GQA_KV_HEADS = 2
GQA_GROUP = GQA_Q_HEADS // GQA_KV_HEADS
GQA_HEAD_DIM = 128
AXIAL_DIM = GQA_HEAD_DIM // 2

FFN_HIDDEN = ((8 * D_MODEL + 3 * 256 - 1) // (3 * 256)) * 256

LANE = 128
SUBLANE = 8
BF16_SUBLANES = 16
MLA_HEAD_PAD = 2 * LANE
ROPE_HALF = 32
V_ROWS = MLA_V + BF16_SUBLANES
LOG2E = math.log2(math.e)
VMEM_LIMIT = 56 * 1024 * 1024

ROW_TILE = 512
Q_TILE = 512
KEY_CHUNK = 256
MXU_WIDTH = 256
_FFN_TILES = FFN_HIDDEN // MXU_WIDTH
FFN_SPLIT = (_FFN_TILES + 1) // 2 * MXU_WIDTH
FFN_CHUNKS = ((0, FFN_SPLIT), (FFN_SPLIT, FFN_HIDDEN))


def _params(n_axes):
    return pltpu.CompilerParams(dimension_semantics=("arbitrary",) * n_axes,
                                vmem_limit_bytes=VMEM_LIMIT)


def _resident(shape):
    nd = len(shape)
    return pl.BlockSpec(shape, lambda *_: (0,) * nd, pipeline_mode=pl.Buffered(1))


def _rms(x, g):
    ms = jnp.mean(x * x, axis=-1, keepdims=True)
    return x * lax.rsqrt(ms + RMS_EPS) * g


def _rms_t(x, g_col):
    ms = jnp.mean(x * x, axis=0, keepdims=True)
    return x * lax.rsqrt(ms + RMS_EPS) * g_col


def _rope_t(x, cos, sin):
    blocks = []
    for lo in range(0, x.shape[0], 2 * ROPE_HALF):
        blocks += [x[lo + ROPE_HALF:lo + 2 * ROPE_HALF], x[lo:lo + ROPE_HALF]]
    return x * cos + jnp.concatenate(blocks, axis=0) * sin


def _ones_rows(n):
    row = lax.broadcasted_iota(jnp.int32, (BF16_SUBLANES, n), 0)
    return jnp.where(row == 0, 1.0, 0.0).astype(BF16)


def _store_q_tiles(qt_ref, h, x):
    tq = qt_ref.shape[-1]
    for j in range(x.shape[1] // tq):
        qt_ref[0, h, j] = x[:, j * tq:(j + 1) * tq]


_NT = (((1,), (1,)), ((), ()))


def _l0_proj_kernel(h_ref, gmix_ref, wmt_ref, wconv_ref, qag_ref, kvag_ref, wuqt_ref, wukt_ref,
                    wuvt_ref, qg_ref, kg_ref, cos_ref, sin_ref,
                    qt_ref, k_ref, vt_ref, gb_ref, cu_ref):
    hn = _rms(h_ref[...], gmix_ref[...]).astype(BF16)
    tm = hn.shape[0]
    zc = jnp.dot(hn, wconv_ref[...], preferred_element_type=F32)
    gb_ref[...] = zc[:, :CONV_DIM]
    cu_ref[...] = zc[:, CONV_DIM:2 * CONV_DIM] * zc[:, 2 * CONV_DIM:]

    zt = lax.dot_general(wmt_ref[...], hn, _NT, preferred_element_type=F32)
    cq = _rms_t(zt[:MLA_Q_RANK], qag_ref[...]).astype(BF16)
    ckv = _rms_t(zt[MLA_Q_RANK:MLA_Q_RANK + MLA_KV_RANK], kvag_ref[...]).astype(BF16)
    k_r = zt[MLA_Q_RANK + MLA_KV_RANK:]
    qt = jnp.dot(wuqt_ref[...], cq, preferred_element_type=F32)
    knt = jnp.dot(wukt_ref[...], ckv, preferred_element_type=F32)
    vt = jnp.dot(wuvt_ref[...], ckv, preferred_element_type=F32)

    cos, sin = cos_ref[...], sin_ref[...]
    qg, kg = qg_ref[...], kg_ref[...]
    q_scale = MLA_QK ** -0.5 * LOG2E
    kr_sq = jnp.sum(k_r * k_r, axis=0, keepdims=True)
    kr_rot = _rope_t(k_r * kg[MLA_NOPE:], cos, sin)
    zpad = jnp.zeros((MLA_HEAD_PAD - MLA_QK, tm), F32)
    ones = _ones_rows(tm)
    for h in range(MLA_HEADS):
        x = qt[h * MLA_QK:(h + 1) * MLA_QK]
        r = lax.rsqrt(jnp.mean(x * x, axis=0, keepdims=True) + RMS_EPS) * q_scale
        xn = x * r * qg
        qh = jnp.concatenate([xn[:MLA_NOPE], _rope_t(xn[MLA_NOPE:], cos, sin), zpad], axis=0)
        _store_q_tiles(qt_ref, h, qh.astype(BF16))
        kh = knt[h * MLA_NOPE:(h + 1) * MLA_NOPE]
        ms = (jnp.sum(kh * kh, axis=0, keepdims=True) + kr_sq) / MLA_QK
        r = lax.rsqrt(ms + RMS_EPS)
        kt = jnp.concatenate([kh * r * kg[:MLA_NOPE], kr_rot * r, zpad], axis=0)
        k_ref[0, h] = kt.T.astype(BF16)
        vt_ref[0, h, :MLA_V, :] = vt[h * MLA_V:(h + 1) * MLA_V].astype(BF16)
        vt_ref[0, h, MLA_V:, :] = ones


def _qt_spec(n_heads, tm, dk, tq):
    return pl.BlockSpec((1, n_heads, tm // tq, dk, tq), lambda b, j: (b, 0, j, 0, 0))


def _head_spec(n_heads, tm, dk):
    return pl.BlockSpec((1, n_heads, tm, dk), lambda b, j: (b, 0, j, 0))


def _vt_spec(n_heads, tm):
    return pl.BlockSpec((1, n_heads, V_ROWS, tm), lambda b, j: (b, 0, 0, j))


def _l0_proj(h, tabs, w, batch, seq, tm, tq):
    n_j = seq // tm
    row = lambda n: pl.BlockSpec((tm, n), lambda b, j: (b * n_j + j, 0))
    tab = pl.BlockSpec((tabs[0].shape[0], tm), lambda b, j: (0, j))
    ins = [h, w["g_mix"], w["w_mla_t"], w["w_conv"], w["q_a_g"], w["kv_a_g"], w["w_uq_t"],
           w["w_uk_t"], w["w_uv_t"], w["q_g"], w["k_g"], *tabs]
    in_specs = [row(D_MODEL)] + [_resident(a.shape) for a in ins[1:11]] + [tab] * 2
    out_shape = [jax.ShapeDtypeStruct((batch, MLA_HEADS, seq // tq, MLA_HEAD_PAD, tq), BF16),
                 jax.ShapeDtypeStruct((batch, MLA_HEADS, seq, MLA_HEAD_PAD), BF16),
                 jax.ShapeDtypeStruct((batch, MLA_HEADS, V_ROWS, seq), BF16),
                 jax.ShapeDtypeStruct((batch * seq, CONV_DIM), F32),
                 jax.ShapeDtypeStruct((batch * seq, CONV_DIM), F32)]
    out_specs = [_qt_spec(MLA_HEADS, tm, MLA_HEAD_PAD, tq), _head_spec(MLA_HEADS, tm, MLA_HEAD_PAD),
                 _vt_spec(MLA_HEADS, tm), row(CONV_DIM), row(CONV_DIM)]
    return pl.pallas_call(_l0_proj_kernel, grid=(batch, n_j), in_specs=in_specs,
                          out_specs=out_specs, out_shape=out_shape,
                          compiler_params=_params(2), name="l0_proj")(*ins)


def _l1_proj_kernel(h_ref, gmix_ref, wt_ref, qg_ref, kg_ref, cos_ref, sin_ref, qt_ref, k_ref, vt_ref):
    hn = _rms(h_ref[...], gmix_ref[...]).astype(BF16)
    zt = lax.dot_general(wt_ref[...], hn, _NT, preferred_element_type=F32)
    cos, sin = cos_ref[...], sin_ref[...]
    q_scale = GQA_HEAD_DIM ** -0.5 * LOG2E
    ones = _ones_rows(hn.shape[0])
    d = GQA_HEAD_DIM
    for h in range(GQA_Q_HEADS):
        x = _rope_t(_rms_t(zt[h * d:(h + 1) * d], qg_ref[...]), cos, sin) * q_scale
        _store_q_tiles(qt_ref, h, x.astype(BF16))
    for h in range(GQA_KV_HEADS):
        c = (GQA_Q_HEADS + h) * d
        k_ref[0, h] = _rope_t(_rms_t(zt[c:c + d], kg_ref[...]), cos, sin).T.astype(BF16)
        c = (GQA_Q_HEADS + GQA_KV_HEADS + h) * d
        vt_ref[0, h, :d, :] = zt[c:c + d].astype(BF16)
        vt_ref[0, h, d:, :] = ones


def _l1_proj(h, tabs, w, batch, seq, tm, tq):
    n_j = seq // tm
    row = lambda n: pl.BlockSpec((tm, n), lambda b, j: (b * n_j + j, 0))
    tab = pl.BlockSpec((tabs[0].shape[0], tm), lambda b, j: (0, j))
    ins = [h, w["g_mix"], w["w_qkv_t"], w["q_g"], w["k_g"], *tabs]
    in_specs = [row(D_MODEL)] + [_resident(a.shape) for a in ins[1:5]] + [tab] * 2
    out_shape = [jax.ShapeDtypeStruct((batch, GQA_Q_HEADS, seq // tq, GQA_HEAD_DIM, tq), BF16),
                 jax.ShapeDtypeStruct((batch, GQA_KV_HEADS, seq, GQA_HEAD_DIM), BF16),
                 jax.ShapeDtypeStruct((batch, GQA_KV_HEADS, V_ROWS, seq), BF16)]
    out_specs = [_qt_spec(GQA_Q_HEADS, tm, GQA_HEAD_DIM, tq), _head_spec(GQA_KV_HEADS, tm, GQA_HEAD_DIM),
                 _vt_spec(GQA_KV_HEADS, tm)]
    return pl.pallas_call(_l1_proj_kernel, grid=(batch, n_j), in_specs=in_specs,
                          out_specs=out_specs, out_shape=out_shape,
                          compiler_params=_params(2), name="l1_proj")(*ins)


def _meta_heads(x, batch):
    _, n_heads, _, d = x.shape
    return x.reshape(n_heads, batch, N_META, d).transpose(1, 0, 2, 3)


def _meta_vt(vt, batch):
    n_heads = vt.shape[1]
    return vt.reshape(n_heads, V_ROWS, batch, N_META).transpose(2, 0, 1, 3)


def _meta_q_tiles(qt, batch, group, tq):
    n_q, dk = qt.shape[1], qt.shape[3]
    x = qt.reshape(n_q // group, group, dk, batch, N_META).transpose(3, 0, 2, 1, 4)
    x = x.reshape(batch, n_q // group, dk, group * N_META)
    return jnp.pad(x, ((0, 0), (0, 0), (0, 0), (0, tq - group * N_META)))


def _attn_kernel(q_ref, k_ref, vt_ref, qm_ref, km_ref, vtm_ref, o_ref, om_ref, sa_ref, sb_ref, *, kc):
    seq = k_ref.shape[2]
    tq = q_ref.shape[3]
    dv = o_ref.shape[2]
    group = om_ref.shape[1]
    n_tiles = q_ref.shape[1]

    def score_chunk(qt, s_nxt, k_rows, lo, n):
        s = jnp.dot(k_rows, qt, preferred_element_type=F32)
        s_nxt[lo:lo + n, :] = s
        return jnp.max(s.reshape(n // SUBLANE, SUBLANE, tq), axis=0)

    def value_chunk(m, s_cur, vt_cols, lo, n):
        p = jnp.exp2(s_cur[lo:lo + n, :] - m).astype(BF16)
        return jnp.dot(vt_cols, p, preferred_element_type=F32)

    def tile(q_next, s_nxt, m_cur, s_cur):
        m8 = jnp.full((SUBLANE, tq), -jnp.inf, F32)
        acc = jnp.zeros((V_ROWS, tq), F32)
        for lo in range(0, seq, kc):
            if q_next is not None:
                m8 = jnp.maximum(m8, score_chunk(q_next, s_nxt, k_ref[0, 0, lo:lo + kc, :], lo, kc))
            if m_cur is not None:
                acc = acc + value_chunk(m_cur, s_cur, vt_ref[0, 0, :, lo:lo + kc], lo, kc)
        if q_next is not None:
            m8 = jnp.maximum(m8, score_chunk(q_next, s_nxt, km_ref[0, 0], seq, N_META))
        if m_cur is not None:
            acc = acc + value_chunk(m_cur, s_cur, vtm_ref[0, 0], seq, N_META)
        out = None if m_cur is None else (acc[:dv] * (1.0 / acc[dv:dv + 1])).T.astype(o_ref.dtype)
        m_next = None if q_next is None else jnp.max(m8, axis=0, keepdims=True)
        return out, m_next

    def store_tile(t, out):
        o_ref[0, pl.ds(pl.multiple_of(t * tq, tq), tq), :] = out

    _, m_meta = tile(qm_ref[0, 0], sa_ref, None, None)
    out, m0 = tile(q_ref[0, 0], sb_ref, m_meta, sa_ref)
    for g in range(group):
        om_ref[0, g] = out[g * N_META:(g + 1) * N_META]

    def body(i, m):
        out, m = tile(q_ref[0, 2 * i + 1], sa_ref, m, sb_ref)
        store_tile(2 * i, out)
        out, m = tile(q_ref[0, 2 * i + 2], sb_ref, m, sa_ref)
        store_tile(2 * i + 1, out)
        return m

    m = lax.fori_loop(0, n_tiles // 2 - 1, body, m0)
    out, m = tile(q_ref[0, n_tiles - 1], sa_ref, m, sb_ref)
    store_tile(n_tiles - 2, out)
    out, _ = tile(None, None, m, sa_ref)
    store_tile(n_tiles - 1, out)


def _attention(qt, k, vt, qmt, km, vtm, *, group, kc):
    batch, n_q, tiles, dk, tq = qt.shape
    n_kv, seq = k.shape[1], k.shape[2]
    dv = V_ROWS - BF16_SUBLANES
    qt = qt.reshape(batch, n_q * tiles, dk, tq)
    in_specs = [pl.BlockSpec((1, group * tiles, dk, tq), lambda b, h: (b, h, 0, 0)),
                pl.BlockSpec((1, 1, seq, dk), lambda b, h: (b, h, 0, 0)),
                pl.BlockSpec((1, 1, V_ROWS, seq), lambda b, h: (b, h, 0, 0)),
                pl.BlockSpec((1, 1, dk, tq), lambda b, h: (b, h, 0, 0)),
                pl.BlockSpec((1, 1, N_META, dk), lambda b, h: (b, h, 0, 0)),
                pl.BlockSpec((1, 1, V_ROWS, N_META), lambda b, h: (b, h, 0, 0))]
    out_shape = [jax.ShapeDtypeStruct((batch, n_q * seq, dv), BF16),
                 jax.ShapeDtypeStruct((batch, n_q, N_META, dv), BF16)]
    out_specs = [pl.BlockSpec((1, group * seq, dv), lambda b, h: (b, h, 0)),
                 pl.BlockSpec((1, group, N_META, dv), lambda b, h: (b, h, 0, 0))]
    o, om = pl.pallas_call(
        functools.partial(_attn_kernel, kc=kc), grid=(batch, n_kv),
        in_specs=in_specs, out_specs=out_specs, out_shape=out_shape,
        scratch_shapes=[pltpu.VMEM((seq + N_META, tq), F32)] * 2,
        compiler_params=_params(2), name="attention")(qt, k, vt, qmt, km, vtm)
    return o.reshape(batch, n_q, seq, dv), om


def _ffn_tail(h1, gffn_ref, w1_ref, w3_ref, w2_ref):
    hn = _rms(h1, gffn_ref[...]).astype(BF16)
    out = h1
    for lo, hi in FFN_CHUNKS:
        a = jnp.dot(hn, w1_ref[:, lo:hi], preferred_element_type=F32)
        b = jnp.dot(hn, w3_ref[:, lo:hi], preferred_element_type=F32)
        g = (a * jax.nn.sigmoid(a) * b).astype(BF16)
        out = out + jnp.dot(g, w2_ref[lo:hi, :], preferred_element_type=F32)
    return out


def _heads_to_lanes(at_ref):
    return jnp.concatenate([at_ref[0, h] for h in range(at_ref.shape[1])], axis=1)


def _conv_gate(gb, cu, left_edge, right_edge, first, last, cw_ref):
    n = cu.shape[0]
    left = jnp.where(first, left_edge, pltpu.roll(cu, 1, 0))
    right = jnp.where(last, right_edge, pltpu.roll(cu, n - 1, 0))
    conv = left * cw_ref[0:1, :] + cu * cw_ref[1:2, :] + right * cw_ref[2:3, :]
    return gb * conv


def _l0_out_seq_kernel(h_ref, at_ref, gb_ref, cu_ref, prev_ref, next_ref, cum_ref, cw_ref,
                       woa_ref, woc_ref, gffn_ref, w1_ref, w3_ref, w2_ref, o_ref, *, tiles_per_seq):
    j = pl.program_id(0) % tiles_per_seq
    cu = cu_ref[...]
    tm = cu.shape[0]
    row = lax.broadcasted_iota(jnp.int32, (tm, 1), 0)
    left_edge = jnp.where(j == 0, cum_ref[N_META - 1:N_META, :], prev_ref[SUBLANE - 1:SUBLANE, :])
    right_edge = jnp.where(j == tiles_per_seq - 1, 0.0, next_ref[0:1, :])
    y = _conv_gate(gb_ref[...], cu, left_edge, right_edge, row == 0, row == tm - 1, cw_ref)
    mix = (jnp.dot(_heads_to_lanes(at_ref), woa_ref[...], preferred_element_type=F32)
           + jnp.dot(y.astype(BF16), woc_ref[...], preferred_element_type=F32))
    o_ref[...] = _ffn_tail(h_ref[...] + mix, gffn_ref, w1_ref, w3_ref, w2_ref)


def _l0_out_meta_kernel(h_ref, at_ref, gb_ref, cu_ref, nxt_ref, cw_ref,
                        woa_ref, woc_ref, gffn_ref, w1_ref, w3_ref, w2_ref, o_ref):
    cu = cu_ref[...]
    pos = lax.broadcasted_iota(jnp.int32, (cu.shape[0], 1), 0) % N_META
    y = _conv_gate(gb_ref[...], cu, 0.0, nxt_ref[...], pos == 0, pos == N_META - 1, cw_ref)
    mix = (jnp.dot(_heads_to_lanes(at_ref), woa_ref[...], preferred_element_type=F32)
           + jnp.dot(y.astype(BF16), woc_ref[...], preferred_element_type=F32))
    o_ref[...] = _ffn_tail(h_ref[...] + mix, gffn_ref, w1_ref, w3_ref, w2_ref)


def _l1_out_kernel(h_ref, at_ref, wo_ref, gffn_ref, w1_ref, w3_ref, w2_ref, o_ref):
    mix = jnp.dot(_heads_to_lanes(at_ref), wo_ref[...], preferred_element_type=F32)
    o_ref[...] = _ffn_tail(h_ref[...] + mix, gffn_ref, w1_ref, w3_ref, w2_ref)


def _ffn_weights(w):
    return [w["g_ffn"], w["w1"], w["w3"], w["w2"]]


def _attn_in_spec(attn, tm, tiles_per_seq):
    _, n_heads, _, dv = attn.shape
    return pl.BlockSpec((1, n_heads, tm, dv), lambda i: (i // tiles_per_seq, 0, i % tiles_per_seq, 0))


def _l0_out_seq(h, attn, gb, cu, cu_meta, w, *, seq, tm):
    rows = h.shape[0]
    tiles_per_seq = seq // tm
    sub = tm // SUBLANE
    n_sub = rows // SUBLANE
    row = lambda n: pl.BlockSpec((tm, n), lambda i: (i, 0))
    weights = [w["conv_w"], w["w_out_a"], w["w_out_c"]] + _ffn_weights(w)
    in_specs = [row(D_MODEL), _attn_in_spec(attn, tm, tiles_per_seq), row(CONV_DIM), row(CONV_DIM),
                pl.BlockSpec((SUBLANE, CONV_DIM), lambda i: (jnp.maximum(i * sub - 1, 0), 0)),
                pl.BlockSpec((SUBLANE, CONV_DIM), lambda i: (jnp.minimum((i + 1) * sub, n_sub - 1), 0)),
                pl.BlockSpec((N_META, CONV_DIM), lambda i: (i // tiles_per_seq, 0))]
    in_specs += [_resident(a.shape) for a in weights]
    return pl.pallas_call(functools.partial(_l0_out_seq_kernel, tiles_per_seq=tiles_per_seq),
                          grid=(rows // tm,), in_specs=in_specs, out_specs=row(D_MODEL),
                          out_shape=jax.ShapeDtypeStruct((rows, D_MODEL), F32),
                          compiler_params=_params(1), name="l0_out_seq")(
                              h, attn, gb, cu, cu, cu, cu_meta, *weights)


def _l0_out_meta(h, attn, gb, cu, nxt, w):
    rows = h.shape[0]
    weights = [w["conv_w"], w["w_out_a"], w["w_out_c"]] + _ffn_weights(w)
    ins = [h, attn, gb, cu, nxt, *weights]
    return pl.pallas_call(_l0_out_meta_kernel, grid=(1,),
                          in_specs=[_resident(a.shape) for a in ins],
                          out_specs=pl.BlockSpec((rows, D_MODEL), lambda i: (0, 0)),
                          out_shape=jax.ShapeDtypeStruct((rows, D_MODEL), F32),
                          compiler_params=_params(1), name="l0_out_meta")(*ins)


def _l1_out(h, attn, w, *, seq, tm):
    rows = h.shape[0]
    tiles_per_seq = seq // tm
    row = lambda n: pl.BlockSpec((tm, n), lambda i: (i, 0))
    weights = [w["w_out"]] + _ffn_weights(w)
    return pl.pallas_call(_l1_out_kernel, grid=(rows // tm,),
                          in_specs=[row(D_MODEL), _attn_in_spec(attn, tm, tiles_per_seq)]
                          + [_resident(a.shape) for a in weights],
                          out_specs=row(D_MODEL),
                          out_shape=jax.ShapeDtypeStruct((rows, D_MODEL), F32),
                          compiler_params=_params(1), name="l1_out")(h, attn, *weights)


def _meta_attn_rows(om):
    batch, n_heads, _, dv = om.shape
    return om.transpose(1, 0, 2, 3).reshape(1, n_heads, batch * N_META, dv)


def _rope_freqs(dim):
    return ROPE_THETA ** (-jnp.arange(0, dim, 2, dtype=F32) / dim)


def _rope_tables_t(angles):
    cos = jnp.concatenate([jnp.concatenate([jnp.cos(a), jnp.cos(a)], axis=1) for a in angles], axis=1)
    sin = jnp.concatenate([jnp.concatenate([-jnp.sin(a), jnp.sin(a)], axis=1) for a in angles], axis=1)
    return cos.T, sin.T


def _line_tables(seq):
    pos = jnp.arange(N_META + seq, dtype=F32)
    return _rope_tables_t([pos[:, None] * _rope_freqs(MLA_ROPE)[None, :]])


def _axial_tables(seq):
    rows = seq // GRID_W
    f = _rope_freqs(AXIAL_DIM)
    row_idx = jnp.repeat(jnp.arange(rows, dtype=F32), GRID_W)
    col_idx = jnp.tile(jnp.arange(GRID_W, dtype=F32), rows)
    meta = jnp.zeros((N_META, AXIAL_DIM // 2), F32)
    ang_r = jnp.concatenate([meta, row_idx[:, None] * f[None, :]], axis=0)
    ang_c = jnp.concatenate([meta, col_idx[:, None] * f[None, :]], axis=0)
    return _rope_tables_t([ang_r, ang_c])


def _split_tables(tabs, batch):
    seq_t = tuple(t[:, N_META:] for t in tabs)
    meta_t = tuple(jnp.tile(t[:, :N_META], (1, batch)) for t in tabs)
    return seq_t, meta_t


def _col(g):
    return g[:, None]


def _prep_even(i, l, p):
    w_in = p["mla_w_in"][i]
    n_mla = MLA_Q_RANK + MLA_KV_RANK + MLA_ROPE
    w_ukv = p["mla_w_ukv"][i].reshape(MLA_KV_RANK, MLA_HEADS, MLA_NOPE + MLA_V)
    w_out = p["even_w_out"][i]
    return {
        "g_mix": p["mix_norm_g"][l][None, :],
        "w_mla_t": w_in[:, :n_mla].T.astype(BF16), "w_conv": w_in[:, n_mla:].astype(BF16),
        "q_a_g": _col(p["mla_q_a_g"][i]), "kv_a_g": _col(p["mla_kv_a_g"][i]),
        "w_uq_t": p["mla_w_uq"][i].T.astype(BF16),
        "w_uk_t": w_ukv[:, :, :MLA_NOPE].reshape(MLA_KV_RANK, -1).T.astype(BF16),
        "w_uv_t": w_ukv[:, :, MLA_NOPE:].reshape(MLA_KV_RANK, -1).T.astype(BF16),
        "q_g": _col(p["mla_q_g"][i]), "k_g": _col(p["mla_k_g"][i]),
        "conv_w": p["conv_w"][i],
        "w_out_a": w_out[:MLA_HEADS * MLA_V].astype(BF16), "w_out_c": w_out[MLA_HEADS * MLA_V:].astype(BF16),
        "g_ffn": p["ffn_norm_g"][l][None, :], "w1": p["ffn_w1"][l].astype(BF16),
        "w3": p["ffn_w3"][l].astype(BF16), "w2": p["ffn_w2"][l].astype(BF16),
    }


def _prep_odd(i, l, p):
    return {
        "g_mix": p["mix_norm_g"][l][None, :], "w_qkv_t": p["gqa_w_qkv"][i].T.astype(BF16),
        "q_g": _col(p["gqa_q_g"][i]), "k_g": _col(p["gqa_k_g"][i]),
        "w_out": p["odd_w_out"][i].astype(BF16),
        "g_ffn": p["ffn_norm_g"][l][None, :], "w1": p["ffn_w1"][l].astype(BF16),
        "w3": p["ffn_w3"][l].astype(BF16), "w2": p["ffn_w2"][l].astype(BF16),
    }


def _trunk(x, meta_tokens, layers, line_tabs, axial_tabs):
    batch, seq, _ = x.shape
    tm = min(ROW_TILE, seq)
    tq = min(Q_TILE, seq)
    kc = min(KEY_CHUNK, seq)
    hs = x.reshape(batch * seq, D_MODEL)
    hm = jnp.tile(meta_tokens.astype(x.dtype), (batch, 1))
    rows_m = batch * N_META
    for kind, w in layers:
        if kind == "even":
            tabs_s, tabs_m = _split_tables(line_tabs, batch)
            qt, k, vt, gb, cu = _l0_proj(hs, tabs_s, w, batch, seq, tm, tq)
            qmt, km, vtm, gbm, cum = _l0_proj(hm, tabs_m, w, 1, rows_m, rows_m, rows_m)
            o, om = _attention(qt, k, vt, _meta_q_tiles(qmt, batch, 1, tq), _meta_heads(km, batch),
                               _meta_vt(vtm, batch), group=1, kc=kc)
            first = cu.reshape(batch, seq, CONV_DIM)[:, 0]
            nxt = jnp.zeros((batch, N_META, CONV_DIM), F32).at[:, N_META - 1].set(first)
            hs_new = _l0_out_seq(hs, o, gb, cu, cum, w, seq=seq, tm=tm)
            hm = _l0_out_meta(hm, _meta_attn_rows(om), gbm, cum, nxt.reshape(rows_m, CONV_DIM), w)
            hs = hs_new
        else:
            tabs_s, tabs_m = _split_tables(axial_tabs, batch)
            qt, k, vt = _l1_proj(hs, tabs_s, w, batch, seq, tm, tq)
            qmt, km, vtm = _l1_proj(hm, tabs_m, w, 1, rows_m, rows_m, rows_m)
            o, om = _attention(qt, k, vt, _meta_q_tiles(qmt, batch, GQA_GROUP, tq), _meta_heads(km, batch),
                               _meta_vt(vtm, batch), group=GQA_GROUP, kc=kc)
            hs = _l1_out(hs, o, w, seq=seq, tm=tm)
            hm = _l1_out(hm, _meta_attn_rows(om), w, seq=rows_m, tm=rows_m)
    return hs.reshape(batch, seq, D_MODEL)


def kernel(x_prompt, x_sample, meta_tokens, mix_norm_g, ffn_norm_g, mla_w_in, mla_q_a_g, mla_kv_a_g, mla_w_uq, mla_w_ukv, mla_q_g, mla_k_g, conv_w, even_w_out, gqa_w_qkv, gqa_q_g, gqa_k_g, odd_w_out, ffn_w1, ffn_w3, ffn_w2):
    p = dict(mix_norm_g=mix_norm_g, ffn_norm_g=ffn_norm_g, mla_w_in=mla_w_in, mla_q_a_g=mla_q_a_g,
             mla_kv_a_g=mla_kv_a_g, mla_w_uq=mla_w_uq, mla_w_ukv=mla_w_ukv, mla_q_g=mla_q_g,
             mla_k_g=mla_k_g, conv_w=conv_w, even_w_out=even_w_out, gqa_w_qkv=gqa_w_qkv,
             gqa_q_g=gqa_q_g, gqa_k_g=gqa_k_g, odd_w_out=odd_w_out, ffn_w1=ffn_w1, ffn_w3=ffn_w3,
             ffn_w2=ffn_w2)
    depth = mix_norm_g.shape[0]
    layers = [("even", _prep_even(l // 2, l, p)) if l % 2 == 0 else ("odd", _prep_odd(l // 2, l, p))
              for l in range(depth)]
    outs = []
    for x in (x_prompt, x_sample):
        seq = x.shape[1]
        outs.append(_trunk(x, meta_tokens, layers, _line_tables(seq), _axial_tables(seq)))
    return tuple(outs)
```

```python
import functools
import math

import jax
import jax.numpy as jnp
from jax import lax
from jax.experimental import pallas as pl
from jax.experimental.pallas import tpu as pltpu

F32 = jnp.float32
BF16 = jnp.bfloat16

D_MODEL = 1024
N_META = 16
GRID_W = 64
RMS_EPS = 1e-6
ROPE_THETA = 10000.0

MLA_HEADS = 4
MLA_Q_RANK = 384
MLA_KV_RANK = 256
MLA_NOPE = 128
MLA_ROPE = 64
MLA_V = 128
MLA_QK = MLA_NOPE + MLA_ROPE
CONV_DIM = D_MODEL // 2
CONV_WIDTH = 3

GQA_Q_HEADS = 8
GQA_KV_HEADS = 2
GQA_GROUP = GQA_Q_HEADS // GQA_KV_HEADS
GQA_HEAD_DIM = 128
AXIAL_DIM = GQA_HEAD_DIM // 2

FFN_HIDDEN = ((8 * D_MODEL + 3 * 256 - 1) // (3 * 256)) * 256

LANE = 128
SUBLANE = 8
BF16_SUBLANES = 16
MLA_HEAD_PAD = 2 * LANE
ROPE_HALF = 32
V_ROWS = MLA_V + BF16_SUBLANES
LOG2E = math.log2(math.e)
VMEM_LIMIT = 56 * 1024 * 1024

ROW_TILE = 512
Q_TILE = 512
META_TILE = 256
KEY_CHUNK = 256
MXU_WIDTH = 256
_FFN_TILES = FFN_HIDDEN // MXU_WIDTH
FFN_SPLIT = (_FFN_TILES + 1) // 2 * MXU_WIDTH
FFN_CHUNKS = ((0, FFN_SPLIT), (FFN_SPLIT, FFN_HIDDEN))


def _params(n_axes):
    return pltpu.CompilerParams(dimension_semantics=("arbitrary",) * n_axes,
                                vmem_limit_bytes=VMEM_LIMIT)


def _resident(shape):
    nd = len(shape)
    return pl.BlockSpec(shape, lambda *_: (0,) * nd, pipeline_mode=pl.Buffered(1))


def _rms(x, g):
    ms = jnp.mean(x * x, axis=-1, keepdims=True)
    return x * lax.rsqrt(ms + RMS_EPS) * g


def _rms_t(x, g_col):
    ms = jnp.mean(x * x, axis=0, keepdims=True)
    return x * lax.rsqrt(ms + RMS_EPS) * g_col


def _rope_t(x, cos, sin):
    blocks = []
    for lo in range(0, x.shape[0], 2 * ROPE_HALF):
        blocks += [x[lo + ROPE_HALF:lo + 2 * ROPE_HALF], x[lo:lo + ROPE_HALF]]
    return x * cos + jnp.concatenate(blocks, axis=0) * sin


def _ones_rows(n):
    row = lax.broadcasted_iota(jnp.int32, (BF16_SUBLANES, n), 0)
    return jnp.where(row == 0, 1.0, 0.0).astype(BF16)


def _store_q_tiles(qt_ref, h, x):
    tq = qt_ref.shape[-1]
    for j in range(x.shape[1] // tq):
        qt_ref[0, h, j] = x[:, j * tq:(j + 1) * tq]


_NT = (((1,), (1,)), ((), ()))


def _l0_proj_kernel(h_ref, gmix_ref, wmt_ref, wconv_ref, qag_ref, kvag_ref, wuqt_ref, wukt_ref,
                    wuvt_ref, qg_ref, kg_ref, cos_ref, sin_ref,
                    qt_ref, k_ref, vt_ref, gb_ref, cu_ref):
    hn = _rms(h_ref[...], gmix_ref[...]).astype(BF16)
    tm = hn.shape[0]
    zc = jnp.dot(hn, wconv_ref[...], preferred_element_type=F32)
    gb_ref[...] = zc[:, :CONV_DIM]
    cu_ref[...] = zc[:, CONV_DIM:2 * CONV_DIM] * zc[:, 2 * CONV_DIM:]

    zt = lax.dot_general(wmt_ref[...], hn, _NT, preferred_element_type=F32)
    cq = _rms_t(zt[:MLA_Q_RANK], qag_ref[...]).astype(BF16)
    ckv = _rms_t(zt[MLA_Q_RANK:MLA_Q_RANK + MLA_KV_RANK], kvag_ref[...]).astype(BF16)
    k_r = zt[MLA_Q_RANK + MLA_KV_RANK:]
    qt = jnp.dot(wuqt_ref[...], cq, preferred_element_type=F32)
    knt = jnp.dot(wukt_ref[...], ckv, preferred_element_type=F32)
    vt = jnp.dot(wuvt_ref[...], ckv, preferred_element_type=F32)

    cos, sin = cos_ref[...], sin_ref[...]
    qg, kg = qg_ref[...], kg_ref[...]
    q_scale = MLA_QK ** -0.5 * LOG2E
    kr_sq = jnp.sum(k_r * k_r, axis=0, keepdims=True)
    kr_rot = _rope_t(k_r * kg[MLA_NOPE:], cos, sin)
    zpad = jnp.zeros((MLA_HEAD_PAD - MLA_QK, tm), F32)
    ones = _ones_rows(tm)
    for h in range(MLA_HEADS):
        x = qt[h * MLA_QK:(h + 1) * MLA_QK]
        r = lax.rsqrt(jnp.mean(x * x, axis=0, keepdims=True) + RMS_EPS) * q_scale
        xn = x * r * qg
        qh = jnp.concatenate([xn[:MLA_NOPE], _rope_t(xn[MLA_NOPE:], cos, sin), zpad], axis=0)
        _store_q_tiles(qt_ref, h, qh.astype(BF16))
        kh = knt[h * MLA_NOPE:(h + 1) * MLA_NOPE]
        ms = (jnp.sum(kh * kh, axis=0, keepdims=True) + kr_sq) / MLA_QK
        r = lax.rsqrt(ms + RMS_EPS)
        kt = jnp.concatenate([kh * r * kg[:MLA_NOPE], kr_rot * r, zpad], axis=0)
        k_ref[0, h] = kt.T.astype(BF16)
        vt_ref[0, h, :MLA_V, :] = vt[h * MLA_V:(h + 1) * MLA_V].astype(BF16)
        vt_ref[0, h, MLA_V:, :] = ones


def _qt_spec(n_heads, tm, dk, tq):
    return pl.BlockSpec((1, n_heads, tm // tq, dk, tq), lambda b, j: (b, 0, j, 0, 0))


def _head_spec(n_heads, tm, dk):
    return pl.BlockSpec((1, n_heads, tm, dk), lambda b, j: (b, 0, j, 0))


def _vt_spec(n_heads, tm):
    return pl.BlockSpec((1, n_heads, V_ROWS, tm), lambda b, j: (b, 0, 0, j))


def _l0_proj(h, tabs, w, batch, seq, tm, tq):
    n_j = seq // tm
    row = lambda n: pl.BlockSpec((tm, n), lambda b, j: (b * n_j + j, 0))
    tab = pl.BlockSpec((tabs[0].shape[0], tm), lambda b, j: (0, j))
    ins = [h, w["g_mix"], w["w_mla_t"], w["w_conv"], w["q_a_g"], w["kv_a_g"], w["w_uq_t"],
           w["w_uk_t"], w["w_uv_t"], w["q_g"], w["k_g"], *tabs]
    in_specs = [row(D_MODEL)] + [_resident(a.shape) for a in ins[1:11]] + [tab] * 2
    out_shape = [jax.ShapeDtypeStruct((batch, MLA_HEADS, seq // tq, MLA_HEAD_PAD, tq), BF16),
                 jax.ShapeDtypeStruct((batch, MLA_HEADS, seq, MLA_HEAD_PAD), BF16),
                 jax.ShapeDtypeStruct((batch, MLA_HEADS, V_ROWS, seq), BF16),
                 jax.ShapeDtypeStruct((batch * seq, CONV_DIM), F32),
                 jax.ShapeDtypeStruct((batch * seq, CONV_DIM), F32)]
    out_specs = [_qt_spec(MLA_HEADS, tm, MLA_HEAD_PAD, tq), _head_spec(MLA_HEADS, tm, MLA_HEAD_PAD),
                 _vt_spec(MLA_HEADS, tm), row(CONV_DIM), row(CONV_DIM)]
    return pl.pallas_call(_l0_proj_kernel, grid=(batch, n_j), in_specs=in_specs,
                          out_specs=out_specs, out_shape=out_shape,
                          compiler_params=_params(2), name="l0_proj")(*ins)


def _l1_proj_kernel(h_ref, gmix_ref, wt_ref, qg_ref, kg_ref, cos_ref, sin_ref, qt_ref, k_ref, vt_ref):
    hn = _rms(h_ref[...], gmix_ref[...]).astype(BF16)
    zt = lax.dot_general(wt_ref[...], hn, _NT, preferred_element_type=F32)
    cos, sin = cos_ref[...], sin_ref[...]
    q_scale = GQA_HEAD_DIM ** -0.5 * LOG2E
    ones = _ones_rows(hn.shape[0])
    d = GQA_HEAD_DIM
    for h in range(GQA_Q_HEADS):
        x = _rope_t(_rms_t(zt[h * d:(h + 1) * d], qg_ref[...]), cos, sin) * q_scale
        _store_q_tiles(qt_ref, h, x.astype(BF16))
    for h in range(GQA_KV_HEADS):
        c = (GQA_Q_HEADS + h) * d
        k_ref[0, h] = _rope_t(_rms_t(zt[c:c + d], kg_ref[...]), cos, sin).T.astype(BF16)
        c = (GQA_Q_HEADS + GQA_KV_HEADS + h) * d
        vt_ref[0, h, :d, :] = zt[c:c + d].astype(BF16)
        vt_ref[0, h, d:, :] = ones


def _l1_proj(h, tabs, w, batch, seq, tm, tq):
    n_j = seq // tm
    row = lambda n: pl.BlockSpec((tm, n), lambda b, j: (b * n_j + j, 0))
    tab = pl.BlockSpec((tabs[0].shape[0], tm), lambda b, j: (0, j))
    ins = [h, w["g_mix"], w["w_qkv_t"], w["q_g"], w["k_g"], *tabs]
    in_specs = [row(D_MODEL)] + [_resident(a.shape) for a in ins[1:5]] + [tab] * 2
    out_shape = [jax.ShapeDtypeStruct((batch, GQA_Q_HEADS, seq // tq, GQA_HEAD_DIM, tq), BF16),
                 jax.ShapeDtypeStruct((batch, GQA_KV_HEADS, seq, GQA_HEAD_DIM), BF16),
                 jax.ShapeDtypeStruct((batch, GQA_KV_HEADS, V_ROWS, seq), BF16)]
    out_specs = [_qt_spec(GQA_Q_HEADS, tm, GQA_HEAD_DIM, tq), _head_spec(GQA_KV_HEADS, tm, GQA_HEAD_DIM),
                 _vt_spec(GQA_KV_HEADS, tm)]
    return pl.pallas_call(_l1_proj_kernel, grid=(batch, n_j), in_specs=in_specs,
                          out_specs=out_specs, out_shape=out_shape,
                          compiler_params=_params(2), name="l1_proj")(*ins)


def _meta_heads(x, batch):
    _, n_heads, _, d = x.shape
    return x.reshape(n_heads, batch, N_META, d).transpose(1, 0, 2, 3)


def _meta_vt(vt, batch):
    n_heads = vt.shape[1]
    return vt.reshape(n_heads, V_ROWS, batch, N_META).transpose(2, 0, 1, 3)


def _meta_q_tiles(qt, batch, group, tq):
    n_q, dk = qt.shape[1], qt.shape[3]
    x = qt.reshape(n_q // group, group, dk, batch, N_META).transpose(3, 0, 2, 1, 4)
    x = x.reshape(batch, n_q // group, dk, group * N_META)
    return jnp.pad(x, ((0, 0), (0, 0), (0, 0), (0, tq - group * N_META)))


def _attn_kernel(q_ref, k_ref, vt_ref, qm_ref, km_ref, vtm_ref, o_ref, om_ref, sa_ref, sb_ref, acc_ref,
                 *, kc):
    seq = k_ref.shape[2]
    tq = q_ref.shape[3]
    dv = o_ref.shape[2]
    group = om_ref.shape[1]
    n_tiles = q_ref.shape[1]

    def score_chunk(qt, s_nxt, k_rows, lo, n):
        w = qt.shape[1]
        s = jnp.dot(k_rows, qt, preferred_element_type=F32)
        s_nxt[lo:lo + n, :w] = s
        return jnp.max(s.reshape(n // SUBLANE, SUBLANE, w), axis=0)

    def value_chunk(m, s_cur, vt_cols, lo, n):
        p = jnp.exp2(s_cur[lo:lo + n, :m.shape[1]] - m).astype(BF16)
        return jnp.dot(vt_cols, p, preferred_element_type=F32)

    def tile(q_next, s_nxt, m_cur, s_cur):
        m8 = None if q_next is None else jnp.full((SUBLANE, q_next.shape[1]), -jnp.inf, F32)
        acc = None if m_cur is None else jnp.zeros((V_ROWS, m_cur.shape[1]), F32)
        for lo in range(0, seq, kc):
            if q_next is not None:
                m8 = jnp.maximum(m8, score_chunk(q_next, s_nxt, k_ref[0, 0, lo:lo + kc, :], lo, kc))
            if m_cur is not None:
                acc = acc + value_chunk(m_cur, s_cur, vt_ref[0, 0, :, lo:lo + kc], lo, kc)
        if q_next is not None:
            m8 = jnp.maximum(m8, score_chunk(q_next, s_nxt, km_ref[0, 0], seq, N_META))
        if m_cur is not None:
            acc = acc + value_chunk(m_cur, s_cur, vtm_ref[0, 0], seq, N_META)
        m_next = None if q_next is None else jnp.max(m8, axis=0, keepdims=True)
        return acc, m_next

    def normalise(acc):
        return (acc[:dv] * (1.0 / acc[dv:dv + 1])).T.astype(o_ref.dtype)

    def emit(t, acc):
        o_ref[0, pl.ds(pl.multiple_of(t * tq, tq), tq), :] = normalise(acc)

    _, m = tile(qm_ref[0, 0], sa_ref, None, None)
    acc, m = tile(q_ref[0, 0], sb_ref, m, sa_ref)
    out = normalise(acc)
    for g in range(group):
        om_ref[0, g] = out[g * N_META:(g + 1) * N_META]
    acc_ref[...], m = tile(q_ref[0, 1], sa_ref, m, sb_ref)

    def body(i, m):
        emit(2 * i, acc_ref[...])
        acc, m = tile(q_ref[0, 2 * i + 2], sb_ref, m, sa_ref)
        emit(2 * i + 1, acc)
        acc_ref[...], m = tile(q_ref[0, 2 * i + 3], sa_ref, m, sb_ref)
        return m

    m = lax.fori_loop(0, n_tiles // 2 - 1, body, m)
    emit(n_tiles - 2, acc_ref[...])
    acc, _ = tile(None, None, m, sa_ref)
    emit(n_tiles - 1, acc)


def _attention(qt, k, vt, qmt, km, vtm, *, group, kc):
    batch, n_q, tiles, dk, tq = qt.shape
    n_kv, seq = k.shape[1], k.shape[2]
    dv = V_ROWS - BF16_SUBLANES
    qt = qt.reshape(batch, n_q * tiles, dk, tq)
    in_specs = [pl.BlockSpec((1, group * tiles, dk, tq), lambda b, h: (b, h, 0, 0)),
                pl.BlockSpec((1, 1, seq, dk), lambda b, h: (b, h, 0, 0)),
                pl.BlockSpec((1, 1, V_ROWS, seq), lambda b, h: (b, h, 0, 0)),
                pl.BlockSpec((1, 1, dk, qmt.shape[3]), lambda b, h: (b, h, 0, 0)),
                pl.BlockSpec((1, 1, N_META, dk), lambda b, h: (b, h, 0, 0)),
                pl.BlockSpec((1, 1, V_ROWS, N_META), lambda b, h: (b, h, 0, 0))]
    out_shape = [jax.ShapeDtypeStruct((batch, n_q * seq, dv), BF16),
                 jax.ShapeDtypeStruct((batch, n_q, N_META, dv), BF16)]
    out_specs = [pl.BlockSpec((1, group * seq, dv), lambda b, h: (b, h, 0)),
                 pl.BlockSpec((1, group, N_META, dv), lambda b, h: (b, h, 0, 0))]
    o, om = pl.pallas_call(
        functools.partial(_attn_kernel, kc=kc), grid=(batch, n_kv),
        in_specs=in_specs, out_specs=out_specs, out_shape=out_shape,
        scratch_shapes=[pltpu.VMEM((seq + N_META, tq), F32)] * 2 + [pltpu.VMEM((V_ROWS, tq), F32)],
        compiler_params=_params(2), name="attention")(qt, k, vt, qmt, km, vtm)
    return o.reshape(batch, n_q, seq, dv), om


def _ffn_tail(h1, gffn_ref, w1_ref, w3_ref, w2_ref):
    hn = _rms(h1, gffn_ref[...]).astype(BF16)
    out = h1
    for lo, hi in FFN_CHUNKS:
        a = jnp.dot(hn, w1_ref[:, lo:hi], preferred_element_type=F32)
        b = jnp.dot(hn, w3_ref[:, lo:hi], preferred_element_type=F32)
        g = (a * jax.nn.sigmoid(a) * b).astype(BF16)
        out = out + jnp.dot(g, w2_ref[lo:hi, :], preferred_element_type=F32)
    return out


def _heads_to_lanes(at_ref):
    return jnp.concatenate([at_ref[0, h] for h in range(at_ref.shape[1])], axis=1)


def _conv_gate(gb, cu, left_edge, right_edge, first, last, cw_ref):
    n = cu.shape[0]
    left = jnp.where(first, left_edge, pltpu.roll(cu, 1, 0))
    right = jnp.where(last, right_edge, pltpu.roll(cu, n - 1, 0))
    conv = left * cw_ref[0:1, :] + cu * cw_ref[1:2, :] + right * cw_ref[2:3, :]
    return gb * conv


def _l0_out_seq_kernel(h_ref, at_ref, gb_ref, cu_ref, prev_ref, next_ref, cum_ref, cw_ref,
                       woa_ref, woc_ref, gffn_ref, w1_ref, w3_ref, w2_ref, o_ref, *, tiles_per_seq):
    j = pl.program_id(0) % tiles_per_seq
    cu = cu_ref[...]
    tm = cu.shape[0]
    row = lax.broadcasted_iota(jnp.int32, (tm, 1), 0)
    left_edge = jnp.where(j == 0, cum_ref[N_META - 1:N_META, :], prev_ref[SUBLANE - 1:SUBLANE, :])
    right_edge = jnp.where(j == tiles_per_seq - 1, 0.0, next_ref[0:1, :])
    y = _conv_gate(gb_ref[...], cu, left_edge, right_edge, row == 0, row == tm - 1, cw_ref)
    mix = (jnp.dot(_heads_to_lanes(at_ref), woa_ref[...], preferred_element_type=F32)
           + jnp.dot(y.astype(BF16), woc_ref[...], preferred_element_type=F32))
    o_ref[...] = _ffn_tail(h_ref[...] + mix, gffn_ref, w1_ref, w3_ref, w2_ref)


def _l0_out_meta_kernel(h_ref, at_ref, gb_ref, cu_ref, nxt_ref, cw_ref,
                        woa_ref, woc_ref, gffn_ref, w1_ref, w3_ref, w2_ref, o_ref):
    cu = cu_ref[...]
    pos = lax.broadcasted_iota(jnp.int32, (cu.shape[0], 1), 0) % N_META
    y = _conv_gate(gb_ref[...], cu, 0.0, nxt_ref[...], pos == 0, pos == N_META - 1, cw_ref)
    mix = (jnp.dot(_heads_to_lanes(at_ref), woa_ref[...], preferred_element_type=F32)
           + jnp.dot(y.astype(BF16), woc_ref[...], preferred_element_type=F32))
    o_ref[...] = _ffn_tail(h_ref[...] + mix, gffn_ref, w1_ref, w3_ref, w2_ref)


def _l1_out_kernel(h_ref, at_ref, wo_ref, gffn_ref, w1_ref, w3_ref, w2_ref, o_ref):
    mix = jnp.dot(_heads_to_lanes(at_ref), wo_ref[...], preferred_element_type=F32)
    o_ref[...] = _ffn_tail(h_ref[...] + mix, gffn_ref, w1_ref, w3_ref, w2_ref)


def _ffn_weights(w):
    return [w["g_ffn"], w["w1"], w["w3"], w["w2"]]


def _attn_in_spec(attn, tm, tiles_per_seq):
    _, n_heads, _, dv = attn.shape
    return pl.BlockSpec((1, n_heads, tm, dv), lambda i: (i // tiles_per_seq, 0, i % tiles_per_seq, 0))


def _l0_out_seq(h, attn, gb, cu, cu_meta, w, *, seq, tm):
    rows = h.shape[0]
    tiles_per_seq = seq // tm
    sub = tm // SUBLANE
    n_sub = rows // SUBLANE
    row = lambda n: pl.BlockSpec((tm, n), lambda i: (i, 0))
    weights = [w["conv_w"], w["w_out_a"], w["w_out_c"]] + _ffn_weights(w)
    in_specs = [row(D_MODEL), _attn_in_spec(attn, tm, tiles_per_seq), row(CONV_DIM), row(CONV_DIM),
                pl.BlockSpec((SUBLANE, CONV_DIM), lambda i: (jnp.maximum(i * sub - 1, 0), 0)),
                pl.BlockSpec((SUBLANE, CONV_DIM), lambda i: (jnp.minimum((i + 1) * sub, n_sub - 1), 0)),
                pl.BlockSpec((N_META, CONV_DIM), lambda i: (i // tiles_per_seq, 0))]
    in_specs += [_resident(a.shape) for a in weights]
    return pl.pallas_call(functools.partial(_l0_out_seq_kernel, tiles_per_seq=tiles_per_seq),
                          grid=(rows // tm,), in_specs=in_specs, out_specs=row(D_MODEL),
                          out_shape=jax.ShapeDtypeStruct((rows, D_MODEL), F32),
                          compiler_params=_params(1), name="l0_out_seq")(
                              h, attn, gb, cu, cu, cu, cu_meta, *weights)


def _l0_out_meta(h, attn, gb, cu, nxt, w):
    rows = h.shape[0]
    weights = [w["conv_w"], w["w_out_a"], w["w_out_c"]] + _ffn_weights(w)
    ins = [h, attn, gb, cu, nxt, *weights]
    return pl.pallas_call(_l0_out_meta_kernel, grid=(1,),
                          in_specs=[_resident(a.shape) for a in ins],
                          out_specs=pl.BlockSpec((rows, D_MODEL), lambda i: (0, 0)),
                          out_shape=jax.ShapeDtypeStruct((rows, D_MODEL), F32),
                          compiler_params=_params(1), name="l0_out_meta")(*ins)


def _l1_out(h, attn, w, *, seq, tm):
    rows = h.shape[0]
    tiles_per_seq = seq // tm
    row = lambda n: pl.BlockSpec((tm, n), lambda i: (i, 0))
    weights = [w["w_out"]] + _ffn_weights(w)
    return pl.pallas_call(_l1_out_kernel, grid=(rows // tm,),
                          in_specs=[row(D_MODEL), _attn_in_spec(attn, tm, tiles_per_seq)]
                          + [_resident(a.shape) for a in weights],
                          out_specs=row(D_MODEL),
                          out_shape=jax.ShapeDtypeStruct((rows, D_MODEL), F32),
                          compiler_params=_params(1), name="l1_out")(h, attn, *weights)


def _meta_attn_rows(om):
    batch, n_heads, _, dv = om.shape
    return om.transpose(1, 0, 2, 3).reshape(1, n_heads, batch * N_META, dv)


def _rope_freqs(dim):
    return ROPE_THETA ** (-jnp.arange(0, dim, 2, dtype=F32) / dim)


def _rope_tables_t(angles):
    cos = jnp.concatenate([jnp.concatenate([jnp.cos(a), jnp.cos(a)], axis=1) for a in angles], axis=1)
    sin = jnp.concatenate([jnp.concatenate([-jnp.sin(a), jnp.sin(a)], axis=1) for a in angles], axis=1)
    return cos.T, sin.T


def _line_tables(seq):
    pos = jnp.arange(N_META + seq, dtype=F32)
    return _rope_tables_t([pos[:, None] * _rope_freqs(MLA_ROPE)[None, :]])


def _axial_tables(seq):
    rows = seq // GRID_W
    f = _rope_freqs(AXIAL_DIM)
    row_idx = jnp.repeat(jnp.arange(rows, dtype=F32), GRID_W)
    col_idx = jnp.tile(jnp.arange(GRID_W, dtype=F32), rows)
    meta = jnp.zeros((N_META, AXIAL_DIM // 2), F32)
    ang_r = jnp.concatenate([meta, row_idx[:, None] * f[None, :]], axis=0)
    ang_c = jnp.concatenate([meta, col_idx[:, None] * f[None, :]], axis=0)
    return _rope_tables_t([ang_r, ang_c])


def _split_tables(tabs, batch):
    seq_t = tuple(t[:, N_META:] for t in tabs)
    meta_t = tuple(jnp.tile(t[:, :N_META], (1, batch)) for t in tabs)
    return seq_t, meta_t


def _col(g):
    return g[:, None]


def _prep_even(i, l, p):
    w_in = p["mla_w_in"][i]
    n_mla = MLA_Q_RANK + MLA_KV_RANK + MLA_ROPE
    w_ukv = p["mla_w_ukv"][i].reshape(MLA_KV_RANK, MLA_HEADS, MLA_NOPE + MLA_V)
    w_out = p["even_w_out"][i]
    return {
        "g_mix": p["mix_norm_g"][l][None, :],
        "w_mla_t": w_in[:, :n_mla].T.astype(BF16), "w_conv": w_in[:, n_mla:].astype(BF16),
        "q_a_g": _col(p["mla_q_a_g"][i]), "kv_a_g": _col(p["mla_kv_a_g"][i]),
        "w_uq_t": p["mla_w_uq"][i].T.astype(BF16),
        "w_uk_t": w_ukv[:, :, :MLA_NOPE].reshape(MLA_KV_RANK, -1).T.astype(BF16),
        "w_uv_t": w_ukv[:, :, MLA_NOPE:].reshape(MLA_KV_RANK, -1).T.astype(BF16),
        "q_g": _col(p["mla_q_g"][i]), "k_g": _col(p["mla_k_g"][i]),
        "conv_w": p["conv_w"][i],
        "w_out_a": w_out[:MLA_HEADS * MLA_V].astype(BF16), "w_out_c": w_out[MLA_HEADS * MLA_V:].astype(BF16),
        "g_ffn": p["ffn_norm_g"][l][None, :], "w1": p["ffn_w1"][l].astype(BF16),
        "w3": p["ffn_w3"][l].astype(BF16), "w2": p["ffn_w2"][l].astype(BF16),
    }


def _prep_odd(i, l, p):
    return {
        "g_mix": p["mix_norm_g"][l][None, :], "w_qkv_t": p["gqa_w_qkv"][i].T.astype(BF16),
        "q_g": _col(p["gqa_q_g"][i]), "k_g": _col(p["gqa_k_g"][i]),
        "w_out": p["odd_w_out"][i].astype(BF16),
        "g_ffn": p["ffn_norm_g"][l][None, :], "w1": p["ffn_w1"][l].astype(BF16),
        "w3": p["ffn_w3"][l].astype(BF16), "w2": p["ffn_w2"][l].astype(BF16),
    }


def _trunk(x, meta_tokens, layers, line_tabs, axial_tabs):
    batch, seq, _ = x.shape
    tm = min(ROW_TILE, seq)
    tq = min(Q_TILE, seq)
    tqm = min(META_TILE, tq)
    kc = min(KEY_CHUNK, seq)
    hs = x.reshape(batch * seq, D_MODEL)
    hm = jnp.tile(meta_tokens.astype(x.dtype), (batch, 1))
    rows_m = batch * N_META
    for kind, w in layers:
        if kind == "even":
            tabs_s, tabs_m = _split_tables(line_tabs, batch)
            qt, k, vt, gb, cu = _l0_proj(hs, tabs_s, w, batch, seq, tm, tq)
            qmt, km, vtm, gbm, cum = _l0_proj(hm, tabs_m, w, 1, rows_m, rows_m, rows_m)
            o, om = _attention(qt, k, vt, _meta_q_tiles(qmt, batch, 1, tqm), _meta_heads(km, batch),
                               _meta_vt(vtm, batch), group=1, kc=kc)
            first = cu.reshape(batch, seq, CONV_DIM)[:, 0]
            nxt = jnp.zeros((batch, N_META, CONV_DIM), F32).at[:, N_META - 1].set(first)
            hs_new = _l0_out_seq(hs, o, gb, cu, cum, w, seq=seq, tm=tm)
            hm = _l0_out_meta(hm, _meta_attn_rows(om), gbm, cum, nxt.reshape(rows_m, CONV_DIM), w)
            hs = hs_new
        else:
            tabs_s, tabs_m = _split_tables(axial_tabs, batch)
            qt, k, vt = _l1_proj(hs, tabs_s, w, batch, seq, tm, tq)
            qmt, km, vtm = _l1_proj(hm, tabs_m, w, 1, rows_m, rows_m, rows_m)
            o, om = _attention(qt, k, vt, _meta_q_tiles(qmt, batch, GQA_GROUP, tqm), _meta_heads(km, batch),
                               _meta_vt(vtm, batch), group=GQA_GROUP, kc=kc)
            hs = _l1_out(hs, o, w, seq=seq, tm=tm)
            hm = _l1_out(hm, _meta_attn_rows(om), w, seq=rows_m, tm=rows_m)
    return hs.reshape(batch, seq, D_MODEL)


def kernel(x_prompt, x_sample, meta_tokens, mix_norm_g, ffn_norm_g, mla_w_in, mla_q_a_g, mla_kv_a_g, mla_w_uq, mla_w_ukv, mla_q_g, mla_k_g, conv_w, even_w_out, gqa_w_qkv, gqa_q_g, gqa_k_g, odd_w_out, ffn_w1, ffn_w3, ffn_w2):
    p = dict(mix_norm_g=mix_norm_g, ffn_norm_g=ffn_norm_g, mla_w_in=mla_w_in, mla_q_a_g=mla_q_a_g,
             mla_kv_a_g=mla_kv_a_g, mla_w_uq=mla_w_uq, mla_w_ukv=mla_w_ukv, mla_q_g=mla_q_g,
             mla_k_g=mla_k_g, conv_w=conv_w, even_w_out=even_w_out, gqa_w_qkv=gqa_w_qkv,
             gqa_q_g=gqa_q_g, gqa_k_g=gqa_k_g, odd_w_out=odd_w_out, ffn_w1=ffn_w1, ffn_w3=ffn_w3,
             ffn_w2=ffn_w2)
    depth = mix_norm_g.shape[0]
    layers = [("even", _prep_even(l // 2, l, p)) if l % 2 == 0 else ("odd", _prep_odd(l // 2, l, p))
              for l in range(depth)]
    outs = []
    for x in (x_prompt, x_sample):
        seq = x.shape[1]
        outs.append(_trunk(x, meta_tokens, layers, _line_tables(seq), _axial_tables(seq)))
    return tuple(outs)
```

```python
import functools
import math

import jax
import jax.numpy as jnp
from jax import lax
from jax.experimental import pallas as pl
from jax.experimental.pallas import tpu as pltpu

F32 = jnp.float32
BF16 = jnp.bfloat16

D_MODEL = 1024
N_META = 16
GRID_W = 64
RMS_EPS = 1e-6
ROPE_THETA = 10000.0

MLA_HEADS = 4
MLA_Q_RANK = 384
MLA_KV_RANK = 256
MLA_NOPE = 128
MLA_ROPE = 64
MLA_V = 128
MLA_QK = MLA_NOPE + MLA_ROPE
CONV_DIM = D_MODEL // 2
CONV_WIDTH = 3

GQA_Q_HEADS = 8
GQA_KV_HEADS = 2
GQA_GROUP = GQA_Q_HEADS // GQA_KV_HEADS
GQA_HEAD_DIM = 128
AXIAL_DIM = GQA_HEAD_DIM // 2

FFN_HIDDEN = ((8 * D_MODEL + 3 * 256 - 1) // (3 * 256)) * 256

LANE = 128
SUBLANE = 8
BF16_SUBLANES = 16
MLA_HEAD_PAD = 2 * LANE
ROPE_HALF = 32
V_ROWS = MLA_V + BF16_SUBLANES
LOG2E = math.log2(math.e)
VMEM_LIMIT = 56 * 1024 * 1024

ROW_TILE = 512
PROJ_TILE = 1024
PROJ_HEADS_PER_DOT = 4
Q_TILE = 512
META_TILE = 256
KEY_CHUNK = 256
MXU_WIDTH = 256
_FFN_TILES = FFN_HIDDEN // MXU_WIDTH
FFN_SPLIT = (_FFN_TILES + 1) // 2 * MXU_WIDTH
FFN_CHUNKS = ((0, FFN_SPLIT), (FFN_SPLIT, FFN_HIDDEN))


def _params(n_axes):
    return pltpu.CompilerParams(dimension_semantics=("arbitrary",) * n_axes,
                                vmem_limit_bytes=VMEM_LIMIT)


def _resident(shape):
    nd = len(shape)
    return pl.BlockSpec(shape, lambda *_: (0,) * nd, pipeline_mode=pl.Buffered(1))


def _rms(x, g):
    ms = jnp.mean(x * x, axis=-1, keepdims=True)
    return x * lax.rsqrt(ms + RMS_EPS) * g


def _rms_t(x, g_col):
    ms = jnp.mean(x * x, axis=0, keepdims=True)
    return x * lax.rsqrt(ms + RMS_EPS) * g_col


def _rope_t(x, cos, sin):
    blocks = []
    for lo in range(0, x.shape[0], 2 * ROPE_HALF):
        blocks += [x[lo + ROPE_HALF:lo + 2 * ROPE_HALF], x[lo:lo + ROPE_HALF]]
    return x * cos + jnp.concatenate(blocks, axis=0) * sin


def _ones_rows(n):
    row = lax.broadcasted_iota(jnp.int32, (BF16_SUBLANES, n), 0)
    return jnp.where(row == 0, 1.0, 0.0).astype(BF16)


def _store_q_tiles(qt_ref, h, x):
    tq = qt_ref.shape[-1]
    for j in range(x.shape[1] // tq):
        qt_ref[0, h, j] = x[:, j * tq:(j + 1) * tq]


_NT = (((1,), (1,)), ((), ()))


def _l0_proj_kernel(h_ref, gmix_ref, wmt_ref, wconv_ref, qag_ref, kvag_ref, wuqt_ref, wukt_ref,
                    wuvt_ref, qg_ref, kg_ref, cos_ref, sin_ref,
                    qt_ref, k_ref, vt_ref, gb_ref, cu_ref):
    hn = _rms(h_ref[...], gmix_ref[...]).astype(BF16)
    tm = hn.shape[0]
    zc = jnp.dot(hn, wconv_ref[...], preferred_element_type=F32)
    gb_ref[...] = zc[:, :CONV_DIM]
    cu_ref[...] = zc[:, CONV_DIM:2 * CONV_DIM] * zc[:, 2 * CONV_DIM:]

    zt = lax.dot_general(wmt_ref[...], hn, _NT, preferred_element_type=F32)
    cq = _rms_t(zt[:MLA_Q_RANK], qag_ref[...]).astype(BF16)
    ckv = _rms_t(zt[MLA_Q_RANK:MLA_Q_RANK + MLA_KV_RANK], kvag_ref[...]).astype(BF16)
    k_r = zt[MLA_Q_RANK + MLA_KV_RANK:]
    qt = jnp.dot(wuqt_ref[...], cq, preferred_element_type=F32)
    knt = jnp.dot(wukt_ref[...], ckv, preferred_element_type=F32)
    vt = jnp.dot(wuvt_ref[...], ckv, preferred_element_type=F32)

    cos, sin = cos_ref[...], sin_ref[...]
    qg, kg = qg_ref[...], kg_ref[...]
    q_scale = MLA_QK ** -0.5 * LOG2E
    kr_sq = jnp.sum(k_r * k_r, axis=0, keepdims=True)
    kr_rot = _rope_t(k_r * kg[MLA_NOPE:], cos, sin)
    zpad = jnp.zeros((MLA_HEAD_PAD - MLA_QK, tm), F32)
    ones = _ones_rows(tm)
    for h in range(MLA_HEADS):
        x = qt[h * MLA_QK:(h + 1) * MLA_QK]
        r = lax.rsqrt(jnp.mean(x * x, axis=0, keepdims=True) + RMS_EPS) * q_scale
        xn = x * r * qg
        qh = jnp.concatenate([xn[:MLA_NOPE], _rope_t(xn[MLA_NOPE:], cos, sin), zpad], axis=0)
        _store_q_tiles(qt_ref, h, qh.astype(BF16))
        kh = knt[h * MLA_NOPE:(h + 1) * MLA_NOPE]
        ms = (jnp.sum(kh * kh, axis=0, keepdims=True) + kr_sq) / MLA_QK
        r = lax.rsqrt(ms + RMS_EPS)
        kt = jnp.concatenate([kh * r * kg[:MLA_NOPE], kr_rot * r, zpad], axis=0)
        k_ref[0, h] = kt.T.astype(BF16)
        vt_ref[0, h, :MLA_V, :] = vt[h * MLA_V:(h + 1) * MLA_V].astype(BF16)
        vt_ref[0, h, MLA_V:, :] = ones


def _qt_spec(n_heads, tm, dk, tq):
    return pl.BlockSpec((1, n_heads, tm // tq, dk, tq), lambda b, j: (b, 0, j, 0, 0))


def _head_spec(n_heads, tm, dk):
    return pl.BlockSpec((1, n_heads, tm, dk), lambda b, j: (b, 0, j, 0))


def _vt_spec(n_heads, tm):
    return pl.BlockSpec((1, n_heads, V_ROWS, tm), lambda b, j: (b, 0, 0, j))


def _l0_proj(h, tabs, w, batch, seq, tm, tq):
    n_j = seq // tm
    row = lambda n: pl.BlockSpec((tm, n), lambda b, j: (b * n_j + j, 0))
    tab = pl.BlockSpec((tabs[0].shape[0], tm), lambda b, j: (0, j))
    ins = [h, w["g_mix"], w["w_mla_t"], w["w_conv"], w["q_a_g"], w["kv_a_g"], w["w_uq_t"],
           w["w_uk_t"], w["w_uv_t"], w["q_g"], w["k_g"], *tabs]
    in_specs = [row(D_MODEL)] + [_resident(a.shape) for a in ins[1:11]] + [tab] * 2
    out_shape = [jax.ShapeDtypeStruct((batch, MLA_HEADS, seq // tq, MLA_HEAD_PAD, tq), BF16),
                 jax.ShapeDtypeStruct((batch, MLA_HEADS, seq, MLA_HEAD_PAD), BF16),
                 jax.ShapeDtypeStruct((batch, MLA_HEADS, V_ROWS, seq), BF16),
                 jax.ShapeDtypeStruct((batch * seq, CONV_DIM), F32),
                 jax.ShapeDtypeStruct((batch * seq, CONV_DIM), F32)]
    out_specs = [_qt_spec(MLA_HEADS, tm, MLA_HEAD_PAD, tq), _head_spec(MLA_HEADS, tm, MLA_HEAD_PAD),
                 _vt_spec(MLA_HEADS, tm), row(CONV_DIM), row(CONV_DIM)]
    return pl.pallas_call(_l0_proj_kernel, grid=(batch, n_j), in_specs=in_specs,
                          out_specs=out_specs, out_shape=out_shape,
                          compiler_params=_params(2), name="l0_proj")(*ins)


def _l1_proj_kernel(h_ref, gmix_ref, wt_ref, qg_ref, kg_ref, cos_ref, sin_ref, qt_ref, k_ref, vt_ref):
    hn = _rms(h_ref[...], gmix_ref[...]).astype(BF16)
    cos, sin = cos_ref[...], sin_ref[...]
    q_scale = GQA_HEAD_DIM ** -0.5 * LOG2E
    ones = _ones_rows(hn.shape[0])
    d = GQA_HEAD_DIM
    per_dot = PROJ_HEADS_PER_DOT

    def project(g):
        rows = wt_ref[g * per_dot * d:(g + 1) * per_dot * d, :]
        return lax.dot_general(rows, hn, _NT, preferred_element_type=F32)

    def finish(g, zt):
        for i in range(per_dot):
            h = g * per_dot + i
            x = zt[i * d:(i + 1) * d]
            if h < GQA_Q_HEADS:
                x = _rope_t(_rms_t(x, qg_ref[...]), cos, sin) * q_scale
                _store_q_tiles(qt_ref, h, x.astype(BF16))
            elif h < GQA_Q_HEADS + GQA_KV_HEADS:
                k_ref[0, h - GQA_Q_HEADS] = _rope_t(_rms_t(x, kg_ref[...]), cos, sin).T.astype(BF16)
            else:
                hv = h - GQA_Q_HEADS - GQA_KV_HEADS
                vt_ref[0, hv, :d, :] = x.astype(BF16)
                vt_ref[0, hv, d:, :] = ones

    n_groups = (GQA_Q_HEADS + 2 * GQA_KV_HEADS) // per_dot
    pending = project(0)
    for g in range(1, n_groups):
        nxt = project(g)
        finish(g - 1, pending)
        pending = nxt
    finish(n_groups - 1, pending)


def _l1_proj(h, tabs, w, batch, seq, tm, tq):
    n_j = seq // tm
    row = lambda n: pl.BlockSpec((tm, n), lambda b, j: (b * n_j + j, 0))
    tab = pl.BlockSpec((tabs[0].shape[0], tm), lambda b, j: (0, j))
    ins = [h, w["g_mix"], w["w_qkv_t"], w["q_g"], w["k_g"], *tabs]
    in_specs = [row(D_MODEL)] + [_resident(a.shape) for a in ins[1:5]] + [tab] * 2
    out_shape = [jax.ShapeDtypeStruct((batch, GQA_Q_HEADS, seq // tq, GQA_HEAD_DIM, tq), BF16),
                 jax.ShapeDtypeStruct((batch, GQA_KV_HEADS, seq, GQA_HEAD_DIM), BF16),
                 jax.ShapeDtypeStruct((batch, GQA_KV_HEADS, V_ROWS, seq), BF16)]
    out_specs = [_qt_spec(GQA_Q_HEADS, tm, GQA_HEAD_DIM, tq), _head_spec(GQA_KV_HEADS, tm, GQA_HEAD_DIM),
                 _vt_spec(GQA_KV_HEADS, tm)]
    return pl.pallas_call(_l1_proj_kernel, grid=(batch, n_j), in_specs=in_specs,
                          out_specs=out_specs, out_shape=out_shape,
                          compiler_params=_params(2), name="l1_proj")(*ins)


def _meta_heads(x, batch):
    _, n_heads, _, d = x.shape
    return x.reshape(n_heads, batch, N_META, d).transpose(1, 0, 2, 3)


def _meta_vt(vt, batch):
    n_heads = vt.shape[1]
    return vt.reshape(n_heads, V_ROWS, batch, N_META).transpose(2, 0, 1, 3)


def _meta_q_tiles(qt, batch, group, tq):
    n_q, dk = qt.shape[1], qt.shape[3]
    x = qt.reshape(n_q // group, group, dk, batch, N_META).transpose(3, 0, 2, 1, 4)
    x = x.reshape(batch, n_q // group, dk, group * N_META)
    return jnp.pad(x, ((0, 0), (0, 0), (0, 0), (0, tq - group * N_META)))


def _attn_kernel(q_ref, k_ref, vt_ref, qm_ref, km_ref, vtm_ref, o_ref, om_ref, sa_ref, sb_ref, acc_ref,
                 *, kc):
    seq = k_ref.shape[2]
    tq = q_ref.shape[3]
    dv = o_ref.shape[2]
    group = om_ref.shape[1]
    n_tiles = q_ref.shape[1]

    def score_chunk(qt, s_nxt, k_rows, lo, n):
        w = qt.shape[1]
        s = jnp.dot(k_rows, qt, preferred_element_type=F32)
        s_nxt[lo:lo + n, :w] = s
        return jnp.max(s.reshape(n // SUBLANE, SUBLANE, w), axis=0)

    def value_chunk(m, s_cur, vt_cols, lo, n):
        p = jnp.exp2(s_cur[lo:lo + n, :m.shape[1]] - m).astype(BF16)
        return jnp.dot(vt_cols, p, preferred_element_type=F32)

    def tile(q_next, s_nxt, m_cur, s_cur):
        m8 = None if q_next is None else jnp.full((SUBLANE, q_next.shape[1]), -jnp.inf, F32)
        acc = None if m_cur is None else jnp.zeros((V_ROWS, m_cur.shape[1]), F32)
        for lo in range(0, seq, kc):
            if q_next is not None:
                m8 = jnp.maximum(m8, score_chunk(q_next, s_nxt, k_ref[0, 0, lo:lo + kc, :], lo, kc))
            if m_cur is not None:
                acc = acc + value_chunk(m_cur, s_cur, vt_ref[0, 0, :, lo:lo + kc], lo, kc)
        if q_next is not None:
            m8 = jnp.maximum(m8, score_chunk(q_next, s_nxt, km_ref[0, 0], seq, N_META))
        if m_cur is not None:
            acc = acc + value_chunk(m_cur, s_cur, vtm_ref[0, 0], seq, N_META)
        m_next = None if q_next is None else jnp.max(m8, axis=0, keepdims=True)
        return acc, m_next

    def normalise(acc):
        return (acc[:dv] * (1.0 / acc[dv:dv + 1])).T.astype(o_ref.dtype)

    def emit(t, acc):
        o_ref[0, pl.ds(pl.multiple_of(t * tq, tq), tq), :] = normalise(acc)

    _, m = tile(qm_ref[0, 0], sa_ref, None, None)
    acc, m = tile(q_ref[0, 0], sb_ref, m, sa_ref)
    out = normalise(acc)
    for g in range(group):
        om_ref[0, g] = out[g * N_META:(g + 1) * N_META]
    acc_ref[...], m = tile(q_ref[0, 1], sa_ref, m, sb_ref)

    def body(i, m):
        emit(2 * i, acc_ref[...])
        acc, m = tile(q_ref[0, 2 * i + 2], sb_ref, m, sa_ref)
        emit(2 * i + 1, acc)
        acc_ref[...], m = tile(q_ref[0, 2 * i + 3], sa_ref, m, sb_ref)
        return m

    m = lax.fori_loop(0, n_tiles // 2 - 1, body, m)
    emit(n_tiles - 2, acc_ref[...])
    acc, _ = tile(None, None, m, sa_ref)
    emit(n_tiles - 1, acc)


def _attention(qt, k, vt, qmt, km, vtm, *, group, kc):
    batch, n_q, tiles, dk, tq = qt.shape
    n_kv, seq = k.shape[1], k.shape[2]
    dv = V_ROWS - BF16_SUBLANES
    qt = qt.reshape(batch, n_q * tiles, dk, tq)
    in_specs = [pl.BlockSpec((1, group * tiles, dk, tq), lambda b, h: (b, h, 0, 0)),
                pl.BlockSpec((1, 1, seq, dk), lambda b, h: (b, h, 0, 0)),
                pl.BlockSpec((1, 1, V_ROWS, seq), lambda b, h: (b, h, 0, 0)),
                pl.BlockSpec((1, 1, dk, qmt.shape[3]), lambda b, h: (b, h, 0, 0)),
                pl.BlockSpec((1, 1, N_META, dk), lambda b, h: (b, h, 0, 0)),
                pl.BlockSpec((1, 1, V_ROWS, N_META), lambda b, h: (b, h, 0, 0))]
    out_shape = [jax.ShapeDtypeStruct((batch, n_q * seq, dv), BF16),
                 jax.ShapeDtypeStruct((batch, n_q, N_META, dv), BF16)]
    out_specs = [pl.BlockSpec((1, group * seq, dv), lambda b, h: (b, h, 0)),
                 pl.BlockSpec((1, group, N_META, dv), lambda b, h: (b, h, 0, 0))]
    o, om = pl.pallas_call(
        functools.partial(_attn_kernel, kc=kc), grid=(batch, n_kv),
        in_specs=in_specs, out_specs=out_specs, out_shape=out_shape,
        scratch_shapes=[pltpu.VMEM((seq + N_META, tq), F32)] * 2 + [pltpu.VMEM((V_ROWS, tq), F32)],
        compiler_params=_params(2), name="attention")(qt, k, vt, qmt, km, vtm)
    return o.reshape(batch, n_q, seq, dv), om


def _ffn_tail(h1, gffn_ref, w1_ref, w3_ref, w2_ref):
    hn = _rms(h1, gffn_ref[...]).astype(BF16)
    out = h1
    for lo, hi in FFN_CHUNKS:
        a = jnp.dot(hn, w1_ref[:, lo:hi], preferred_element_type=F32)
        b = jnp.dot(hn, w3_ref[:, lo:hi], preferred_element_type=F32)
        g = (a * jax.nn.sigmoid(a) * b).astype(BF16)
        out = out + jnp.dot(g, w2_ref[lo:hi, :], preferred_element_type=F32)
    return out


def _heads_to_lanes(at_ref):
    return jnp.concatenate([at_ref[0, h] for h in range(at_ref.shape[1])], axis=1)


def _conv_gate(gb, cu, left_edge, right_edge, first, last, cw_ref):
    n = cu.shape[0]
    left = jnp.where(first, left_edge, pltpu.roll(cu, 1, 0))
    right = jnp.where(last, right_edge, pltpu.roll(cu, n - 1, 0))
    conv = left * cw_ref[0:1, :] + cu * cw_ref[1:2, :] + right * cw_ref[2:3, :]
    return gb * conv


def _l0_out_seq_kernel(h_ref, at_ref, gb_ref, cu_ref, prev_ref, next_ref, cum_ref, cw_ref,
                       woa_ref, woc_ref, gffn_ref, w1_ref, w3_ref, w2_ref, o_ref, *, tiles_per_seq):
    j = pl.program_id(0) % tiles_per_seq
    cu = cu_ref[...]
    tm = cu.shape[0]
    row = lax.broadcasted_iota(jnp.int32, (tm, 1), 0)
    left_edge = jnp.where(j == 0, cum_ref[N_META - 1:N_META, :], prev_ref[SUBLANE - 1:SUBLANE, :])
    right_edge = jnp.where(j == tiles_per_seq - 1, 0.0, next_ref[0:1, :])
    y = _conv_gate(gb_ref[...], cu, left_edge, right_edge, row == 0, row == tm - 1, cw_ref)
    mix = (jnp.dot(_heads_to_lanes(at_ref), woa_ref[...], preferred_element_type=F32)
           + jnp.dot(y.astype(BF16), woc_ref[...], preferred_element_type=F32))
    o_ref[...] = _ffn_tail(h_ref[...] + mix, gffn_ref, w1_ref, w3_ref, w2_ref)


def _l0_out_meta_kernel(h_ref, at_ref, gb_ref, cu_ref, nxt_ref, cw_ref,
                        woa_ref, woc_ref, gffn_ref, w1_ref, w3_ref, w2_ref, o_ref):
    cu = cu_ref[...]
    pos = lax.broadcasted_iota(jnp.int32, (cu.shape[0], 1), 0) % N_META
    y = _conv_gate(gb_ref[...], cu, 0.0, nxt_ref[...], pos == 0, pos == N_META - 1, cw_ref)
    mix = (jnp.dot(_heads_to_lanes(at_ref), woa_ref[...], preferred_element_type=F32)
           + jnp.dot(y.astype(BF16), woc_ref[...], preferred_element_type=F32))
    o_ref[...] = _ffn_tail(h_ref[...] + mix, gffn_ref, w1_ref, w3_ref, w2_ref)


def _l1_out_kernel(h_ref, at_ref, wo_ref, gffn_ref, w1_ref, w3_ref, w2_ref, o_ref):
    mix = jnp.dot(_heads_to_lanes(at_ref), wo_ref[...], preferred_element_type=F32)
    o_ref[...] = _ffn_tail(h_ref[...] + mix, gffn_ref, w1_ref, w3_ref, w2_ref)


def _ffn_weights(w):
    return [w["g_ffn"], w["w1"], w["w3"], w["w2"]]


def _attn_in_spec(attn, tm, tiles_per_seq):
    _, n_heads, _, dv = attn.shape
    return pl.BlockSpec((1, n_heads, tm, dv), lambda i: (i // tiles_per_seq, 0, i % tiles_per_seq, 0))


def _l0_out_seq(h, attn, gb, cu, cu_meta, w, *, seq, tm):
    rows = h.shape[0]
    tiles_per_seq = seq // tm
    sub = tm // SUBLANE
    n_sub = rows // SUBLANE
    row = lambda n: pl.BlockSpec((tm, n), lambda i: (i, 0))
    weights = [w["conv_w"], w["w_out_a"], w["w_out_c"]] + _ffn_weights(w)
    in_specs = [row(D_MODEL), _attn_in_spec(attn, tm, tiles_per_seq), row(CONV_DIM), row(CONV_DIM),
                pl.BlockSpec((SUBLANE, CONV_DIM), lambda i: (jnp.maximum(i * sub - 1, 0), 0)),
                pl.BlockSpec((SUBLANE, CONV_DIM), lambda i: (jnp.minimum((i + 1) * sub, n_sub - 1), 0)),
                pl.BlockSpec((N_META, CONV_DIM), lambda i: (i // tiles_per_seq, 0))]
    in_specs += [_resident(a.shape) for a in weights]
    return pl.pallas_call(functools.partial(_l0_out_seq_kernel, tiles_per_seq=tiles_per_seq),
                          grid=(rows // tm,), in_specs=in_specs, out_specs=row(D_MODEL),
                          out_shape=jax.ShapeDtypeStruct((rows, D_MODEL), F32),
                          compiler_params=_params(1), name="l0_out_seq")(
                              h, attn, gb, cu, cu, cu, cu_meta, *weights)


def _l0_out_meta(h, attn, gb, cu, nxt, w):
    rows = h.shape[0]
    weights = [w["conv_w"], w["w_out_a"], w["w_out_c"]] + _ffn_weights(w)
    ins = [h, attn, gb, cu, nxt, *weights]
    return pl.pallas_call(_l0_out_meta_kernel, grid=(1,),
                          in_specs=[_resident(a.shape) for a in ins],
                          out_specs=pl.BlockSpec((rows, D_MODEL), lambda i: (0, 0)),
                          out_shape=jax.ShapeDtypeStruct((rows, D_MODEL), F32),
                          compiler_params=_params(1), name="l0_out_meta")(*ins)


def _l1_out(h, attn, w, *, seq, tm):
    rows = h.shape[0]
    tiles_per_seq = seq // tm
    row = lambda n: pl.BlockSpec((tm, n), lambda i: (i, 0))
    weights = [w["w_out"]] + _ffn_weights(w)
    return pl.pallas_call(_l1_out_kernel, grid=(rows // tm,),
                          in_specs=[row(D_MODEL), _attn_in_spec(attn, tm, tiles_per_seq)]
                          + [_resident(a.shape) for a in weights],
                          out_specs=row(D_MODEL),
                          out_shape=jax.ShapeDtypeStruct((rows, D_MODEL), F32),
                          compiler_params=_params(1), name="l1_out")(h, attn, *weights)


def _meta_attn_rows(om):
    batch, n_heads, _, dv = om.shape
    return om.transpose(1, 0, 2, 3).reshape(1, n_heads, batch * N_META, dv)


def _rope_freqs(dim):
    return ROPE_THETA ** (-jnp.arange(0, dim, 2, dtype=F32) / dim)


def _rope_tables_t(angles):
    cos = jnp.concatenate([jnp.concatenate([jnp.cos(a), jnp.cos(a)], axis=1) for a in angles], axis=1)
    sin = jnp.concatenate([jnp.concatenate([-jnp.sin(a), jnp.sin(a)], axis=1) for a in angles], axis=1)
    return cos.T, sin.T


def _line_tables(seq):
    pos = jnp.arange(N_META + seq, dtype=F32)
    return _rope_tables_t([pos[:, None] * _rope_freqs(MLA_ROPE)[None, :]])


def _axial_tables(seq):
    rows = seq // GRID_W
    f = _rope_freqs(AXIAL_DIM)
    row_idx = jnp.repeat(jnp.arange(rows, dtype=F32), GRID_W)
    col_idx = jnp.tile(jnp.arange(GRID_W, dtype=F32), rows)
    meta = jnp.zeros((N_META, AXIAL_DIM // 2), F32)
    ang_r = jnp.concatenate([meta, row_idx[:, None] * f[None, :]], axis=0)
    ang_c = jnp.concatenate([meta, col_idx[:, None] * f[None, :]], axis=0)
    return _rope_tables_t([ang_r, ang_c])


def _split_tables(tabs, batch):
    seq_t = tuple(t[:, N_META:] for t in tabs)
    meta_t = tuple(jnp.tile(t[:, :N_META], (1, batch)) for t in tabs)
    return seq_t, meta_t


def _col(g):
    return g[:, None]


def _prep_even(i, l, p):
    w_in = p["mla_w_in"][i]
    n_mla = MLA_Q_RANK + MLA_KV_RANK + MLA_ROPE
    w_ukv = p["mla_w_ukv"][i].reshape(MLA_KV_RANK, MLA_HEADS, MLA_NOPE + MLA_V)
    w_out = p["even_w_out"][i]
    return {
        "g_mix": p["mix_norm_g"][l][None, :],
        "w_mla_t": w_in[:, :n_mla].T.astype(BF16), "w_conv": w_in[:, n_mla:].astype(BF16),
        "q_a_g": _col(p["mla_q_a_g"][i]), "kv_a_g": _col(p["mla_kv_a_g"][i]),
        "w_uq_t": p["mla_w_uq"][i].T.astype(BF16),
        "w_uk_t": w_ukv[:, :, :MLA_NOPE].reshape(MLA_KV_RANK, -1).T.astype(BF16),
        "w_uv_t": w_ukv[:, :, MLA_NOPE:].reshape(MLA_KV_RANK, -1).T.astype(BF16),
        "q_g": _col(p["mla_q_g"][i]), "k_g": _col(p["mla_k_g"][i]),
        "conv_w": p["conv_w"][i],
        "w_out_a": w_out[:MLA_HEADS * MLA_V].astype(BF16), "w_out_c": w_out[MLA_HEADS * MLA_V:].astype(BF16),
        "g_ffn": p["ffn_norm_g"][l][None, :], "w1": p["ffn_w1"][l].astype(BF16),
        "w3": p["ffn_w3"][l].astype(BF16), "w2": p["ffn_w2"][l].astype(BF16),
    }


def _prep_odd(i, l, p):
    return {
        "g_mix": p["mix_norm_g"][l][None, :], "w_qkv_t": p["gqa_w_qkv"][i].T.astype(BF16),
        "q_g": _col(p["gqa_q_g"][i]), "k_g": _col(p["gqa_k_g"][i]),
        "w_out": p["odd_w_out"][i].astype(BF16),
        "g_ffn": p["ffn_norm_g"][l][None, :], "w1": p["ffn_w1"][l].astype(BF16),
        "w3": p["ffn_w3"][l].astype(BF16), "w2": p["ffn_w2"][l].astype(BF16),
    }


def _trunk(x, meta_tokens, layers, line_tabs, axial_tabs):
    batch, seq, _ = x.shape
    tm = min(ROW_TILE, seq)
    tp = min(PROJ_TILE, seq)
    tq = min(Q_TILE, seq)
    tqm = min(META_TILE, tq)
    kc = min(KEY_CHUNK, seq)
    hs = x.reshape(batch * seq, D_MODEL)
    hm = jnp.tile(meta_tokens.astype(x.dtype), (batch, 1))
    rows_m = batch * N_META
    for kind, w in layers:
        if kind == "even":
            tabs_s, tabs_m = _split_tables(line_tabs, batch)
            qt, k, vt, gb, cu = _l0_proj(hs, tabs_s, w, batch, seq, tp, tq)
            qmt, km, vtm, gbm, cum = _l0_proj(hm, tabs_m, w, 1, rows_m, rows_m, rows_m)
            o, om = _attention(qt, k, vt, _meta_q_tiles(qmt, batch, 1, tqm), _meta_heads(km, batch),
                               _meta_vt(vtm, batch), group=1, kc=kc)
            first = cu.reshape(batch, seq, CONV_DIM)[:, 0]
            nxt = jnp.zeros((batch, N_META, CONV_DIM), F32).at[:, N_META - 1].set(first)
            hs_new = _l0_out_seq(hs, o, gb, cu, cum, w, seq=seq, tm=tm)
            hm = _l0_out_meta(hm, _meta_attn_rows(om), gbm, cum, nxt.reshape(rows_m, CONV_DIM), w)
            hs = hs_new
        else:
            tabs_s, tabs_m = _split_tables(axial_tabs, batch)
            qt, k, vt = _l1_proj(hs, tabs_s, w, batch, seq, tp, tq)
            qmt, km, vtm = _l1_proj(hm, tabs_m, w, 1, rows_m, rows_m, rows_m)
            o, om = _attention(qt, k, vt, _meta_q_tiles(qmt, batch, GQA_GROUP, tqm), _meta_heads(km, batch),
                               _meta_vt(vtm, batch), group=GQA_GROUP, kc=kc)
            hs = _l1_out(hs, o, w, seq=seq, tm=tm)
            hm = _l1_out(hm, _meta_attn_rows(om), w, seq=rows_m, tm=rows_m)
    return hs.reshape(batch, seq, D_MODEL)


def kernel(x_prompt, x_sample, meta_tokens, mix_norm_g, ffn_norm_g, mla_w_in, mla_q_a_g, mla_kv_a_g, mla_w_uq, mla_w_ukv, mla_q_g, mla_k_g, conv_w, even_w_out, gqa_w_qkv, gqa_q_g, gqa_k_g, odd_w_out, ffn_w1, ffn_w3, ffn_w2):
    p = dict(mix_norm_g=mix_norm_g, ffn_norm_g=ffn_norm_g, mla_w_in=mla_w_in, mla_q_a_g=mla_q_a_g,
             mla_kv_a_g=mla_kv_a_g, mla_w_uq=mla_w_uq, mla_w_ukv=mla_w_ukv, mla_q_g=mla_q_g,
             mla_k_g=mla_k_g, conv_w=conv_w, even_w_out=even_w_out, gqa_w_qkv=gqa_w_qkv,
             gqa_q_g=gqa_q_g, gqa_k_g=gqa_k_g, odd_w_out=odd_w_out, ffn_w1=ffn_w1, ffn_w3=ffn_w3,
             ffn_w2=ffn_w2)
    depth = mix_norm_g.shape[0]
    layers = [("even", _prep_even(l // 2, l, p)) if l % 2 == 0 else ("odd", _prep_odd(l // 2, l, p))
              for l in range(depth)]
    outs = []
    for x in (x_prompt, x_sample):
        seq = x.shape[1]
        outs.append(_trunk(x, meta_tokens, layers, _line_tables(seq), _axial_tables(seq)))
    return tuple(outs)
```
